```python
import jax, jax.numpy as jnp
from jax import lax
import numpy as np

D_MODEL = 2048
BATCH = 2
SEQ = 8192
DEPTH = 1

CHUNK = 64
D_MIX = D_MODEL
N_ATTN_HEADS = 16
HEAD_DIM = 64
D_ATTN = N_ATTN_HEADS * HEAD_DIM
D_RNN = D_MIX - D_ATTN
N_RNN_BLOCKS = 16
RNN_BLOCK = D_RNN // N_RNN_BLOCKS
RNN_CONV_W = 4
RG_C = 8.0
D_FF = 3 * D_MODEL
FFN_CONV_W = 3
Q_BLOCK = 128
EPS = 1e-6
D_IN = 3 * D_ATTN + N_ATTN_HEADS + 2 * D_RNN
SPLITS = (D_ATTN, 2 * D_ATTN, 3 * D_ATTN, 3 * D_ATTN + N_ATTN_HEADS, 3 * D_ATTN + N_ATTN_HEADS + D_RNN)

kernel_name = "fox_rglru_hymba_sandwich_block"


def rmsnorm(x, g):
    xf = x.astype(jnp.float32)
    y = xf * lax.rsqrt(jnp.mean(xf * xf, axis=-1, keepdims=True) + EPS)
    return (y * g.astype(jnp.float32)).astype(x.dtype)


def causal_dwconv(x, w, b):
    K = w.shape[0]
    S = x.shape[1]
    xp = jnp.pad(x, ((0, 0), (K - 1, 0), (0, 0)))
    return sum(xp[:, k:k + S] * w[k] for k in range(K)) + b


def forgetting_attention(q, k, v, log_f):
    S = q.shape[1]
    c = jnp.cumsum(log_f, axis=1).transpose(0, 2, 1)
    scale = HEAD_DIM ** -0.5
    outs = []
    for i in range(S // Q_BLOCK):
        q0 = i * Q_BLOCK
        kv_len = q0 + Q_BLOCK
        qb = q[:, q0:kv_len].astype(jnp.float32)
        kb = k[:, :kv_len].astype(jnp.float32)
        vb = v[:, :kv_len].astype(jnp.float32)
        logits = (jnp.einsum('bqhd,bkhd->bhqk', qb, kb) * scale
                  + c[:, :, q0:kv_len, None] - c[:, :, None, :kv_len])
        qpos = q0 + jnp.arange(Q_BLOCK)
        kpos = jnp.arange(kv_len)
        logits = jnp.where(kpos[None, :] <= qpos[:, None], logits, -jnp.inf)
        p = jax.nn.softmax(logits, axis=-1)
        outs.append(jnp.einsum('bhqk,bkhd->bqhd', p, vb).astype(v.dtype))
    return jnp.concatenate(outs, axis=1)


def rg_lru(x, w_a, b_a, w_x, b_x, lam):
    B, S, _ = x.shape
    xf = x.astype(jnp.float32)
    xb = xf.reshape(B, S, N_RNN_BLOCKS, RNN_BLOCK)
    r = jax.nn.sigmoid(jnp.einsum('bsni,nij->bsnj', xb, w_a.astype(jnp.float32)).reshape(B, S, D_RNN) + b_a)
    i_g = jax.nn.sigmoid(jnp.einsum('bsni,nij->bsnj', xb, w_x.astype(jnp.float32)).reshape(B, S, D_RNN) + b_x)
    log_a = -RG_C * r * jax.nn.softplus(-lam.astype(jnp.float32))
    a = jnp.exp(log_a)
    mult = jnp.sqrt(-jnp.expm1(2.0 * log_a))
    mult = jnp.where(jnp.arange(S)[None, :, None] == 0, 1.0, mult)
    bterm = mult * i_g * xf

    def combine(left, right):
        a1, b1 = left
        a2, b2 = right
        return a1 * a2, a2 * b1 + b2

    _, h = lax.associative_scan(combine, (a, bterm), axis=1)
    return h.astype(x.dtype)


def hybrid_mixer(h, w_in, b_forget, conv_rnn_w, conv_rnn_b, w_rg_a, b_rg_a, w_rg_x, b_rg_x,
                 rg_lambda, norm_attn_out, norm_rnn_out, w_out):
    B, S, _ = h.shape
    proj = h @ w_in
    q, k, v, f_logit, x_rnn, g_rnn = jnp.split(proj, SPLITS, axis=-1)
    q = q.reshape(B, S, N_ATTN_HEADS, HEAD_DIM)
    k = k.reshape(B, S, N_ATTN_HEADS, HEAD_DIM)
    v = v.reshape(B, S, N_ATTN_HEADS, HEAD_DIM)
    log_f = jax.nn.log_sigmoid((f_logit + b_forget).astype(jnp.float32))
    y_attn = forgetting_attention(q, k, v, log_f).reshape(B, S, D_ATTN)
    x_rnn = causal_dwconv(x_rnn, conv_rnn_w, conv_rnn_b)
    y_rnn = rg_lru(x_rnn, w_rg_a, b_rg_a, w_rg_x, b_rg_x, rg_lambda) * jax.nn.gelu(g_rnn)
    y = jnp.concatenate([rmsnorm(y_attn, norm_attn_out), rmsnorm(y_rnn, norm_rnn_out)], axis=-1)
    return y @ w_out


def conv_gated_mlp(h, w_gate, w_up, conv_ffn_w, conv_ffn_b, w_down):
    g = causal_dwconv(h @ w_gate, conv_ffn_w, conv_ffn_b)
    return (jax.nn.gelu(g) * (h @ w_up)) @ w_down


def setup_inputs(seed: int = 0) -> dict:
    key = jax.random.key(seed)
    ks = jax.random.split(key, 24)
    f32 = jnp.float32
    nrm = lambda k, shape, s: jax.random.normal(k, shape, f32) * s
    gain = lambda k, shape: 1.0 + 0.02 * jax.random.normal(k, shape, f32)
    L = DEPTH
    u = jax.random.uniform(ks[10], (L, D_RNN), f32, 0.9, 0.999)
    a0 = u ** (1.0 / RG_C)
    rg_lambda = jnp.log(a0) - jnp.log1p(-a0)
    return {
        "x": jax.random.normal(ks[0], (BATCH, SEQ, D_MODEL), f32),
        "norm_mix_pre": gain(ks[1], (L, D_MODEL)),
        "w_in": nrm(ks[2], (L, D_MODEL, D_IN), D_MODEL ** -0.5),
        "b_forget": 2.0 + 0.5 * jax.random.normal(ks[3], (L, N_ATTN_HEADS), f32),
        "conv_rnn_w": nrm(ks[4], (L, RNN_CONV_W, D_RNN), RNN_CONV_W ** -0.5),
        "conv_rnn_b": nrm(ks[5], (L, D_RNN), 0.01),
        "w_rg_a": nrm(ks[6], (L, N_RNN_BLOCKS, RNN_BLOCK, RNN_BLOCK), RNN_BLOCK ** -0.5),
        "b_rg_a": nrm(ks[7], (L, D_RNN), 0.01),
        "w_rg_x": nrm(ks[8], (L, N_RNN_BLOCKS, RNN_BLOCK, RNN_BLOCK), RNN_BLOCK ** -0.5),
        "b_rg_x": nrm(ks[9], (L, D_RNN), 0.01),
        "rg_lambda": rg_lambda,
        "norm_attn_out": gain(ks[11], (L, D_ATTN)),
        "norm_rnn_out": gain(ks[12], (L, D_RNN)),
        "w_out": nrm(ks[13], (L, D_MIX, D_MODEL), D_MIX ** -0.5),
        "norm_mix_post": gain(ks[14], (L, D_MODEL)),
        "norm_ffn_pre": gain(ks[15], (L, D_MODEL)),
        "w_gate": nrm(ks[16], (L, D_MODEL, D_FF), D_MODEL ** -0.5),
        "w_up": nrm(ks[17], (L, D_MODEL, D_FF), D_MODEL ** -0.5),
        "conv_ffn_w": nrm(ks[18], (L, FFN_CONV_W, D_FF), FFN_CONV_W ** -0.5),
        "conv_ffn_b": nrm(ks[19], (L, D_FF), 0.01),
        "w_down": nrm(ks[20], (L, D_FF, D_MODEL), D_FF ** -0.5),
        "norm_ffn_post": gain(ks[21], (L, D_MODEL)),
    }


def reference(x, norm_mix_pre, w_in, b_forget, conv_rnn_w, conv_rnn_b, w_rg_a, b_rg_a, w_rg_x,
              b_rg_x, rg_lambda, norm_attn_out, norm_rnn_out, w_out, norm_mix_post, norm_ffn_pre,
              w_gate, w_up, conv_ffn_w, conv_ffn_b, w_down, norm_ffn_post):
    for l in range(DEPTH):
        h = rmsnorm(x, norm_mix_pre[l])
        y = hybrid_mixer(h, w_in[l], b_forget[l], conv_rnn_w[l], conv_rnn_b[l], w_rg_a[l], b_rg_a[l],
                         w_rg_x[l], b_rg_x[l], rg_lambda[l], norm_attn_out[l], norm_rnn_out[l], w_out[l])
        x = x + rmsnorm(y, norm_mix_post[l])
        h = rmsnorm(x, norm_ffn_pre[l])
        y = conv_gated_mlp(h, w_gate[l], w_up[l], conv_ffn_w[l], conv_ffn_b[l], w_down[l])
        x = x + rmsnorm(y, norm_ffn_post[l])
    return x
```

```python
import functools
import math

import jax
import jax.numpy as jnp
from jax import lax
from jax.experimental import pallas as pl
from jax.experimental.pallas import tpu as pltpu

N_ATTN_HEADS = 16
HEAD_DIM = 64
D_ATTN = N_ATTN_HEADS * HEAD_DIM
N_RNN_BLOCKS = 16
RNN_BLOCK = 64
RNN_CONV_W = 4
RG_C = 8.0
FFN_CONV_W = 3
EPS = 1e-6
LOG2E = 1.4426950408889634

LANES = 128
MXU_DIM = 256
VMEM_LIMIT_BYTES = 56 * 1024 * 1024

F32 = jnp.float32
BF16 = jnp.bfloat16


def _rms(x, gain):
    return x * lax.rsqrt(jnp.mean(x * x, axis=-1, keepdims=True) + EPS) * gain


def _gelu_tanh(x):
    cdf = 0.5 * (1.0 + jnp.tanh(math.sqrt(2.0 / math.pi) * (x + 0.044715 * (x * x * x))))
    return x * cdf


def _softplus(x):
    return jnp.maximum(x, 0.0) + jnp.log1p(jnp.exp(-jnp.abs(x)))


def _one_minus_exp(y, u):
    one_minus = 1.0 - u
    safe_log = jnp.where(u < 1.0, jnp.log(u), -1.0)
    near = one_minus * y / safe_log
    return jnp.where(u == 1.0, -y, jnp.where(u > 0.5, near, one_minus))


def _split3(x):
    hi = x.astype(BF16).astype(F32)
    r = x - hi
    mid = r.astype(BF16).astype(F32)
    lo = (r - mid).astype(BF16).astype(F32)
    return hi, mid, lo


def _dot(a, b):
    return jnp.dot(a, b, preferred_element_type=F32)


def _dot_nt(a, b):
    return lax.dot_general(a, b, (((1,), (1,)), ((), ())), preferred_element_type=F32)


def _proj_kernel(x_ref, g_ref, wcat_ref, wvt_ref, bf_ref,
                 q_ref, k_ref, vt_ref, kb_ref, xr_ref, gr_ref, carry_ref):
    tm = x_ref.shape[1]

    @pl.when(pl.program_id(1) == 0)
    def _():
        carry_ref[...] = jnp.zeros_like(carry_ref)

    h = _rms(x_ref[0], g_ref[...]).astype(BF16)
    q_ref[0] = (_dot(h, wcat_ref[:, 0:D_ATTN]) * (HEAD_DIM ** -0.5 * LOG2E)).astype(BF16)
    k_ref[0] = _dot(h, wcat_ref[:, D_ATTN:2 * D_ATTN]).astype(BF16)
    xr_ref[0] = _dot(h, wcat_ref[:, 2 * D_ATTN:3 * D_ATTN])
    gr_ref[0] = _dot(h, wcat_ref[:, 3 * D_ATTN:4 * D_ATTN])
    vt_ref[0] = _dot_nt(wvt_ref[...], h).astype(BF16)

    fl = _dot(h, wcat_ref[:, 4 * D_ATTN:4 * D_ATTN + LANES]) + bf_ref[...]
    logf = jnp.minimum(fl, 0.0) - jnp.log1p(jnp.exp(-jnp.abs(fl)))
    lane = lax.broadcasted_iota(jnp.int32, (tm, LANES), 1)
    logf = jnp.where(lane < 3 * N_ATTN_HEADS, logf, 0.0)
    hi, mid, lo = _split3(logf)
    pieces = jnp.concatenate([hi, mid, lo], axis=1).astype(BF16)
    rows = lax.broadcasted_iota(jnp.int32, (tm, tm), 0)
    cols = lax.broadcasted_iota(jnp.int32, (tm, tm), 1)
    tri = jnp.where(rows >= cols, 1.0, 0.0).astype(BF16)
    cs = _dot(tri, pieces)
    c = cs[:, 0:LANES] + cs[:, LANES:2 * LANES] + cs[:, 2 * LANES:3 * LANES] + carry_ref[...]
    carry_ref[...] = c[tm - 1:tm, :]
    bhi, bmid, blo = _split3(c * (-LOG2E))
    kb = jnp.where(lane < N_ATTN_HEADS, bhi, jnp.where(lane < 2 * N_ATTN_HEADS, bmid, blo))
    kb_ref[0] = kb.astype(BF16)


def _in_projection(x, gain, wcat, wvt, bf_rep, tm):
    B, S, D = x.shape
    nq = wcat.shape[1]
    const = lambda shape: pl.BlockSpec(shape, lambda b, i: (0,) * len(shape),
                                       pipeline_mode=pl.Buffered(1))
    row = lambda width: pl.BlockSpec((1, tm, width), lambda b, i: (b, i, 0))
    return pl.pallas_call(
        _proj_kernel,
        grid=(B, S // tm),
        in_specs=[row(D), const((1, D)), const((D, nq)), const((D_ATTN, D)), const((1, LANES))],
        out_specs=[row(D_ATTN), row(D_ATTN),
                   pl.BlockSpec((1, D_ATTN, tm), lambda b, i: (b, 0, i)),
                   row(LANES), row(D_ATTN), row(D_ATTN)],
        out_shape=[jax.ShapeDtypeStruct((B, S, D_ATTN), BF16),
                   jax.ShapeDtypeStruct((B, S, D_ATTN), BF16),
                   jax.ShapeDtypeStruct((B, D_ATTN, S), BF16),
                   jax.ShapeDtypeStruct((B, S, LANES), BF16),
                   jax.ShapeDtypeStruct((B, S, D_ATTN), F32),
                   jax.ShapeDtypeStruct((B, S, D_ATTN), F32)],
        scratch_shapes=[pltpu.VMEM((1, LANES), F32)],
        compiler_params=pltpu.CompilerParams(
            dimension_semantics=("arbitrary", "arbitrary"), vmem_limit_bytes=VMEM_LIMIT_BYTES),
        name="in_projection",
    )(x, gain, wcat, wvt, bf_rep)


V_ROWS = HEAD_DIM + 16


def _attn_kernel(q_ref, k_ref, vt_ref, kb_ref, o_ref, rhs_ref, m_ref, acc_ref):
    tq = q_ref.shape[1]
    tk = tq
    hp = pl.program_id(1)
    i = pl.program_id(2)

    q = q_ref[0]
    lane = lax.broadcasted_iota(jnp.int32, (tq, LANES), 1)
    for hh in range(2):
        head = 2 * hp + hh
        qh = jnp.where((lane >= hh * HEAD_DIM) & (lane < (hh + 1) * HEAD_DIM), q, jnp.zeros_like(q))
        sel = (lane == head) | (lane == head + N_ATTN_HEADS) | (lane == head + 2 * N_ATTN_HEADS)
        rhs_ref[hh, :, 0:LANES] = qh
        rhs_ref[hh, :, LANES:2 * LANES] = jnp.where(sel, 1.0, 0.0).astype(BF16)
    m_ref[...] = jnp.full_like(m_ref, -jnp.inf)
    acc_ref[...] = jnp.zeros_like(acc_ref)
    ones = jnp.ones((V_ROWS - HEAD_DIM, tk), BF16)

    def block(j, masked):
        start = pl.multiple_of(j * tk, tk)
        lhs = jnp.concatenate([k_ref[0, pl.ds(start, tk), :], kb_ref[0, pl.ds(start, tk), :]], axis=1)
        for hh in range(2):
            s = _dot_nt(lhs, rhs_ref[hh])
            if masked:
                key = lax.broadcasted_iota(jnp.int32, (tk, tq), 0)
                qry = lax.broadcasted_iota(jnp.int32, (tk, tq), 1)
                s = jnp.where(key <= qry, s, -jnp.inf)
            m_old = m_ref[hh]
            m_new = jnp.maximum(m_old, jnp.max(s, axis=0, keepdims=True))
            p = jnp.exp2(s - m_new).astype(BF16)
            vt = jnp.concatenate(
                [vt_ref[0, hh * HEAD_DIM:(hh + 1) * HEAD_DIM, pl.ds(start, tk)], ones], axis=0)
            acc_ref[hh] = acc_ref[hh] * jnp.exp2(m_old - m_new) + _dot(vt, p)
            m_ref[hh] = m_new

    def body(j, carry):
        block(j, masked=False)
        return carry

    lax.fori_loop(0, i, body, 0)
    block(i, masked=True)

    outs = []
    for hh in range(2):
        acc = acc_ref[hh]
        outs.append(acc[0:HEAD_DIM] * (1.0 / acc[HEAD_DIM:HEAD_DIM + 1]))
    o_ref[0] = jnp.concatenate(outs, axis=0).T.astype(o_ref.dtype)


def _attention(q, k, vt, kb, tq):
    B, S, _ = q.shape
    n_pairs = N_ATTN_HEADS // 2
    return pl.pallas_call(
        _attn_kernel,
        grid=(B, n_pairs, S // tq),
        in_specs=[pl.BlockSpec((1, tq, LANES), lambda b, p, i: (b, i, p)),
                  pl.BlockSpec((1, S, LANES), lambda b, p, i: (b, 0, p)),
                  pl.BlockSpec((1, LANES, S), lambda b, p, i: (b, p, 0)),
                  pl.BlockSpec((1, S, LANES), lambda b, p, i: (b, 0, 0))],
        out_specs=pl.BlockSpec((1, tq, LANES), lambda b, p, i: (b, i, p)),
        out_shape=jax.ShapeDtypeStruct((B, S, D_ATTN), BF16),
        scratch_shapes=[pltpu.VMEM((2, tq, 2 * LANES), BF16),
                        pltpu.VMEM((2, 1, tq), F32),
                        pltpu.VMEM((2, V_ROWS, tq), F32)],
        compiler_params=pltpu.CompilerParams(
            dimension_semantics=("arbitrary", "arbitrary", "arbitrary"),
            vmem_limit_bytes=VMEM_LIMIT_BYTES),
        name="fox_attention",
    )(q, k, vt, kb)


def _rglru_kernel(xr_ref, gr_ref, cw_ref, cb_ref, wbd_ref, ba_ref, bx_ref, lam_ref,
                  y_ref, xbuf_ref, h_ref):
    ts = xr_ref.shape[1]
    width = xr_ref.shape[2]
    i = pl.program_id(1)

    @pl.when(i == 0)
    def _():
        xbuf_ref[0:8, :] = jnp.zeros((8, width), F32)
        h_ref[...] = jnp.zeros_like(h_ref)

    xbuf_ref[8:8 + ts, :] = xr_ref[0]
    xc = cb_ref[...]
    for kk in range(RNN_CONV_W):
        xc = xc + cw_ref[kk:kk + 1, :] * xbuf_ref[pl.ds(8 - (RNN_CONV_W - 1) + kk, ts), :]
    xbuf_ref[0:8, :] = xbuf_ref[ts:ts + 8, :]

    xcb = xc.astype(BF16)
    r_parts, i_parts = [], []
    for g in range(width // MXU_DIM):
        z = _dot(xcb[:, g * MXU_DIM:(g + 1) * MXU_DIM], wbd_ref[g])
        r_parts.append(z[:, 0:MXU_DIM])
        i_parts.append(z[:, MXU_DIM:2 * MXU_DIM])
    r = jax.nn.sigmoid(jnp.concatenate(r_parts, axis=1) + ba_ref[...])
    ig = jax.nn.sigmoid(jnp.concatenate(i_parts, axis=1) + bx_ref[...])
    log_a = (-RG_C) * r * _softplus(-lam_ref[...])
    a = jnp.exp(log_a)
    mult = jnp.sqrt(_one_minus_exp(2.0 * log_a, a * a))
    row = lax.broadcasted_iota(jnp.int32, (ts, width), 0)
    mult = jnp.where(row == jnp.where(i == 0, 0, -1), 1.0, mult)
    b = mult * ig * xc

    d = 1
    while d < ts:
        valid = row >= d
        b = jnp.where(valid, a * pltpu.roll(b, d, axis=0), 0.0) + b
        a = jnp.where(valid, a * pltpu.roll(a, d, axis=0), a)
        d *= 2
    h = a * h_ref[...] + b
    h_ref[...] = h[ts - 1:ts, :]
    y_ref[0] = (h * _gelu_tanh(gr_ref[0])).astype(y_ref.dtype)


def _rglru(xr, gr, cw, cb, wbd, ba, bx, lam, ts):
    B, S, W = xr.shape
    const = lambda shape: pl.BlockSpec(shape, lambda b, i: (0,) * len(shape))
    row = pl.BlockSpec((1, ts, W), lambda b, i: (b, i, 0))
    return pl.pallas_call(
        _rglru_kernel,
        grid=(B, S // ts),
        in_specs=[row, row, const(cw.shape), const((1, W)), const(wbd.shape),
                  const((1, W)), const((1, W)), const((1, W))],
        out_specs=row,
        out_shape=jax.ShapeDtypeStruct((B, S, W), BF16),
        scratch_shapes=[pltpu.VMEM((ts + 8, W), F32), pltpu.VMEM((1, W), F32)],
        compiler_params=pltpu.CompilerParams(
            dimension_semantics=("arbitrary", "arbitrary"), vmem_limit_bytes=VMEM_LIMIT_BYTES),
        name="rg_lru",
    )(xr, gr, cw, cb, wbd, ba, bx, lam)


def _outproj_kernel(ya_ref, yr_ref, x_ref, na_ref, nr_ref, wo_ref, npost_ref, nffn_ref,
                    x1_ref, h2_ref):
    da = ya_ref.shape[1]
    ya = _rms(ya_ref[...].astype(F32), na_ref[...]).astype(BF16)
    yr = _rms(yr_ref[...].astype(F32), nr_ref[...]).astype(BF16)
    y = _dot(ya, wo_ref[0:da, :]) + _dot(yr, wo_ref[da:, :])
    x1 = x_ref[...] + _rms(y, npost_ref[...])
    x1_ref[...] = x1
    h2_ref[...] = _rms(x1, nffn_ref[...]).astype(BF16)


def _out_projection(ya, yr, x, na, nr, wo, npost, nffn, tm):
    M, D = x.shape
    da, dr = ya.shape[1], yr.shape[1]
    const = lambda shape: pl.BlockSpec(shape, lambda i: (0,) * len(shape),
                                       pipeline_mode=pl.Buffered(1))
    row = lambda width: pl.BlockSpec((tm, width), lambda i: (i, 0))
    return pl.pallas_call(
        _outproj_kernel,
        grid=(M // tm,),
        in_specs=[row(da), row(dr), row(D), const((1, da)), const((1, dr)), const(wo.shape),
                  const((1, D)), const((1, D))],
        out_specs=[row(D), row(D)],
        out_shape=[jax.ShapeDtypeStruct((M, D), F32), jax.ShapeDtypeStruct((M, D), BF16)],
        compiler_params=pltpu.CompilerParams(
            dimension_semantics=("arbitrary",), vmem_limit_bytes=VMEM_LIMIT_BYTES),
        name="out_projection",
    )(ya, yr, x, na, nr, wo, npost, nffn)


def _ffn_kernel(h_ref, wg_ref, wu_ref, cw_ref, cb_ref, wd_ref, x1_ref, gain_ref,
                o_ref, acc_ref, gbuf_ref, gtail_ref, *, tiles_per_seq):
    tm = h_ref.shape[0]
    i = pl.program_id(0)
    c = pl.program_id(1)
    h = h_ref[...]
    g = _dot(h, wg_ref[...])
    u = _dot(h, wu_ref[...])
    seq_start = (i % tiles_per_seq) == 0
    gbuf_ref[0:8, :] = jnp.where(seq_start, 0.0, gtail_ref[c])
    gbuf_ref[8:8 + tm, :] = g
    gtail_ref[c] = g[tm - 8:tm, :]
    conv = cb_ref[...] + cw_ref[FFN_CONV_W - 1:FFN_CONV_W, :] * g
    for kk in range(FFN_CONV_W - 1):
        conv = conv + cw_ref[kk:kk + 1, :] * gbuf_ref[pl.ds(8 - (FFN_CONV_W - 1) + kk, tm), :]
    act = (_gelu_tanh(conv) * u).astype(BF16)
    contrib = _dot(act, wd_ref[...])

    @pl.when(c == 0)
    def _():
        acc_ref[...] = contrib

    @pl.when(c > 0)
    def _():
        acc_ref[...] += contrib

    @pl.when(c == pl.num_programs(1) - 1)
    def _():
        o_ref[...] = x1_ref[...] + _rms(acc_ref[...], gain_ref[...])


def _ffn(h2, wg, wu, cw, cb, wd, x1, gain, tm, tc, tiles_per_seq):
    M, D = h2.shape
    dff = wg.shape[1]
    nc = dff // tc
    return pl.pallas_call(
        functools.partial(_ffn_kernel, tiles_per_seq=tiles_per_seq),
        grid=(M // tm, nc),
        in_specs=[pl.BlockSpec((tm, D), lambda i, c: (i, 0)),
                  pl.BlockSpec((D, tc), lambda i, c: (0, c)),
                  pl.BlockSpec((D, tc), lambda i, c: (0, c)),
                  pl.BlockSpec((FFN_CONV_W, tc), lambda i, c: (0, c)),
                  pl.BlockSpec((1, tc), lambda i, c: (0, c)),
                  pl.BlockSpec((tc, D), lambda i, c: (c, 0)),
                  pl.BlockSpec((tm, D), lambda i, c: (i, 0)),
                  pl.BlockSpec((1, D), lambda i, c: (0, 0))],
        out_specs=pl.BlockSpec((tm, D), lambda i, c: (i, 0)),
        out_shape=jax.ShapeDtypeStruct((M, D), F32),
        scratch_shapes=[pltpu.VMEM((tm, D), F32),
                        pltpu.VMEM((tm + 8, tc), F32),
                        pltpu.VMEM((nc, 8, tc), F32)],
        compiler_params=pltpu.CompilerParams(
            dimension_semantics=("arbitrary", "arbitrary"), vmem_limit_bytes=VMEM_LIMIT_BYTES),
        name="conv_gated_mlp",
    )(h2, wg, wu, cw, cb, wd, x1, gain)


def _block_diag(w, group):
    n, r, _ = w.shape
    w = w.reshape(n // group, group, r, r)
    eye = jnp.eye(group, dtype=w.dtype)
    return jnp.einsum("gaij,ab->gaibj", w, eye).reshape(n // group, group * r, group * r)


def _layer(x, norm_mix_pre, w_in, b_forget, conv_rnn_w, conv_rnn_b, w_rg_a, b_rg_a, w_rg_x,
           b_rg_x, rg_lambda, norm_attn_out, norm_rnn_out, w_out, norm_mix_post, norm_ffn_pre,
           w_gate, w_up, conv_ffn_w, conv_ffn_b, w_down, norm_ffn_post):
    B, S, D = x.shape
    d_rnn = conv_rnn_w.shape[1]
    t_seq = min(512, S)

    nh = N_ATTN_HEADS
    w_q, w_k, w_v = w_in[:, 0:D_ATTN], w_in[:, D_ATTN:2 * D_ATTN], w_in[:, 2 * D_ATTN:3 * D_ATTN]
    w_f = w_in[:, 3 * D_ATTN:3 * D_ATTN + nh]
    w_xg = w_in[:, 3 * D_ATTN + nh:]
    w_f_rep = jnp.concatenate([w_f, w_f, w_f, jnp.zeros((D, LANES - 3 * nh), w_in.dtype)], axis=1)
    wcat = jnp.concatenate([w_q, w_k, w_xg, w_f_rep], axis=1).astype(BF16)
    wvt = w_v.T.astype(BF16)
    bf_rep = jnp.concatenate([b_forget, b_forget, b_forget,
                              jnp.zeros((LANES - 3 * nh,), b_forget.dtype)]).reshape(1, LANES)
    group = MXU_DIM // RNN_BLOCK
    wbd = jnp.concatenate([_block_diag(w_rg_a, group), _block_diag(w_rg_x, group)],
                          axis=2).astype(BF16)
    row = lambda v: v.reshape(1, -1)

    q, k, vt, kb, xr, gr = _in_projection(x, row(norm_mix_pre), wcat, wvt, bf_rep, t_seq)
    y_attn = _attention(q, k, vt, kb, t_seq)
    y_rnn = _rglru(xr, gr, conv_rnn_w, row(conv_rnn_b), wbd, row(b_rg_a), row(b_rg_x),
                   row(rg_lambda), t_seq)
    M = B * S
    x1, h2 = _out_projection(y_attn.reshape(M, D_ATTN), y_rnn.reshape(M, d_rnn), x.reshape(M, D),
                             row(norm_attn_out), row(norm_rnn_out), w_out.astype(BF16),
                             row(norm_mix_post), row(norm_ffn_pre), t_seq)
    out = _ffn(h2, w_gate.astype(BF16), w_up.astype(BF16), conv_ffn_w, row(conv_ffn_b),
               w_down.astype(BF16), x1, row(norm_ffn_post), t_seq, 512, S // t_seq)
    return out.reshape(B, S, D)


def kernel(x, norm_mix_pre, w_in, b_forget, conv_rnn_w, conv_rnn_b, w_rg_a, b_rg_a, w_rg_x, b_rg_x, rg_lambda, norm_attn_out, norm_rnn_out, w_out, norm_mix_post, norm_ffn_pre, w_gate, w_up, conv_ffn_w, conv_ffn_b, w_down, norm_ffn_post):
    params = (norm_mix_pre, w_in, b_forget, conv_rnn_w, conv_rnn_b, w_rg_a, b_rg_a, w_rg_x, b_rg_x,
              rg_lambda, norm_attn_out, norm_rnn_out, w_out, norm_mix_post, norm_ffn_pre, w_gate,
              w_up, conv_ffn_w, conv_ffn_b, w_down, norm_ffn_post)
    for l in range(norm_mix_pre.shape[0]):
        x = _layer(x, *(p[l] for p in params))
    return x
```

```python
import functools
import math

import jax
import jax.numpy as jnp
from jax import lax
from jax.experimental import pallas as pl
from jax.experimental.pallas import tpu as pltpu

N_ATTN_HEADS = 16
HEAD_DIM = 64
D_ATTN = N_ATTN_HEADS * HEAD_DIM
N_RNN_BLOCKS = 16
RNN_BLOCK = 64
RNN_CONV_W = 4
RG_C = 8.0
FFN_CONV_W = 3
EPS = 1e-6
LOG2E = 1.4426950408889634

LANES = 128
MXU_DIM = 256
VMEM_LIMIT_BYTES = 56 * 1024 * 1024

F32 = jnp.float32
BF16 = jnp.bfloat16


def _rms(x, gain):
    return x * lax.rsqrt(jnp.mean(x * x, axis=-1, keepdims=True) + EPS) * gain


def _gelu_tanh(x):
    cdf = 0.5 * (1.0 + jnp.tanh(math.sqrt(2.0 / math.pi) * (x + 0.044715 * (x * x * x))))
    return x * cdf


def _softplus(x):
    return jnp.maximum(x, 0.0) + jnp.log1p(jnp.exp(-jnp.abs(x)))


def _split3(x):
    hi = x.astype(BF16).astype(F32)
    r = x - hi
    mid = r.astype(BF16).astype(F32)
    lo = (r - mid).astype(BF16).astype(F32)
    return hi, mid, lo


def _dot(a, b):
    return jnp.dot(a, b, preferred_element_type=F32)


def _dot_nt(a, b):
    return lax.dot_general(a, b, (((1,), (1,)), ((), ())), preferred_element_type=F32)


def _proj_kernel(x_ref, g_ref, wcat_ref, wvt_ref, bf_ref,
                 q_ref, k_ref, vt_ref, kb_ref, xr_ref, gr_ref, carry_ref):
    tm = x_ref.shape[1]

    @pl.when(pl.program_id(1) == 0)
    def _():
        carry_ref[...] = jnp.zeros_like(carry_ref)

    h = _rms(x_ref[0], g_ref[...]).astype(BF16)
    q_ref[0] = (_dot(h, wcat_ref[:, 0:D_ATTN]) * (HEAD_DIM ** -0.5 * LOG2E)).astype(BF16)
    k_ref[0] = _dot(h, wcat_ref[:, D_ATTN:2 * D_ATTN]).astype(BF16)
    xr_ref[0] = _dot(h, wcat_ref[:, 2 * D_ATTN:3 * D_ATTN])
    gr_ref[0] = _dot(h, wcat_ref[:, 3 * D_ATTN:4 * D_ATTN])
    vt_ref[0] = _dot_nt(wvt_ref[...], h).astype(BF16)

    fl = _dot(h, wcat_ref[:, 4 * D_ATTN:4 * D_ATTN + LANES]) + bf_ref[...]
    logf = jnp.minimum(fl, 0.0) - jnp.log1p(jnp.exp(-jnp.abs(fl)))
    lane = lax.broadcasted_iota(jnp.int32, (tm, LANES), 1)
    logf = jnp.where(lane < 3 * N_ATTN_HEADS, logf, 0.0)
    hi, mid, lo = _split3(logf)
    pieces = jnp.concatenate([hi, mid, lo], axis=1).astype(BF16)
    rows = lax.broadcasted_iota(jnp.int32, (tm, tm), 0)
    cols = lax.broadcasted_iota(jnp.int32, (tm, tm), 1)
    tri = jnp.where(rows >= cols, 1.0, 0.0).astype(BF16)
    cs = _dot(tri, pieces)
    c = cs[:, 0:LANES] + cs[:, LANES:2 * LANES] + cs[:, 2 * LANES:3 * LANES] + carry_ref[...]
    carry_ref[...] = c[tm - 1:tm, :]
    bhi, bmid, blo = _split3(c * (-LOG2E))
    kb = jnp.where(lane < N_ATTN_HEADS, bhi, jnp.where(lane < 2 * N_ATTN_HEADS, bmid, blo))
    kb_ref[0] = kb.astype(BF16)


def _in_projection(x, gain, wcat, wvt, bf_rep, tm):
    B, S, D = x.shape
    nq = wcat.shape[1]
    const = lambda shape: pl.BlockSpec(shape, lambda b, i: (0,) * len(shape),
                                       pipeline_mode=pl.Buffered(1))
    row = lambda width: pl.BlockSpec((1, tm, width), lambda b, i: (b, i, 0))
    return pl.pallas_call(
        _proj_kernel,
        grid=(B, S // tm),
        in_specs=[row(D), const((1, D)), const((D, nq)), const((D_ATTN, D)), const((1, LANES))],
        out_specs=[row(D_ATTN), row(D_ATTN),
                   pl.BlockSpec((1, D_ATTN, tm), lambda b, i: (b, 0, i)),
                   row(LANES), row(D_ATTN), row(D_ATTN)],
        out_shape=[jax.ShapeDtypeStruct((B, S, D_ATTN), BF16),
                   jax.ShapeDtypeStruct((B, S, D_ATTN), BF16),
                   jax.ShapeDtypeStruct((B, D_ATTN, S), BF16),
                   jax.ShapeDtypeStruct((B, S, LANES), BF16),
                   jax.ShapeDtypeStruct((B, S, D_ATTN), F32),
                   jax.ShapeDtypeStruct((B, S, D_ATTN), F32)],
        scratch_shapes=[pltpu.VMEM((1, LANES), F32)],
        compiler_params=pltpu.CompilerParams(
            dimension_semantics=("arbitrary", "arbitrary"), vmem_limit_bytes=VMEM_LIMIT_BYTES),
        name="in_projection",
    )(x, gain, wcat, wvt, bf_rep)


V_ROWS = HEAD_DIM + 16


def _attn_kernel(q_ref, k_ref, vt_ref, kb_ref, o_ref,
                 rhs_ref, m_ref, acc_ref, s_a_ref, s_b_ref, mb_a_ref, mb_b_ref, bias_ref):
    tq = q_ref.shape[1]
    tk = tq
    hp = pl.program_id(1)
    i = pl.program_id(2)

    @pl.when((pl.program_id(0) == 0) & (hp == 0) & (i == 0))
    def _():
        key = lax.broadcasted_iota(jnp.int32, (tk, tq), 0)
        qry = lax.broadcasted_iota(jnp.int32, (tk, tq), 1)
        bias_ref[0] = jnp.zeros((tk, tq), F32)
        bias_ref[1] = jnp.where(key <= qry, 0.0, -jnp.inf)

    q = q_ref[0]
    lane = lax.broadcasted_iota(jnp.int32, (tq, LANES), 1)
    for hh in range(2):
        head = 2 * hp + hh
        qh = jnp.where((lane >= hh * HEAD_DIM) & (lane < (hh + 1) * HEAD_DIM), q, jnp.zeros_like(q))
        sel = (lane == head) | (lane == head + N_ATTN_HEADS) | (lane == head + 2 * N_ATTN_HEADS)
        rhs_ref[hh, :, 0:LANES] = qh
        rhs_ref[hh, :, LANES:2 * LANES] = jnp.where(sel, 1.0, 0.0).astype(BF16)
    m_ref[...] = jnp.full_like(m_ref, -jnp.inf)
    acc_ref[...] = jnp.zeros_like(acc_ref)
    ones = jnp.ones((V_ROWS - HEAD_DIM, tk), BF16)

    def scores(j, s_ref, mb_ref):
        start = pl.multiple_of(j * tk, tk)
        lhs = jnp.concatenate([k_ref[0, pl.ds(start, tk), :], kb_ref[0, pl.ds(start, tk), :]], axis=1)
        bias = bias_ref[jnp.where(j == i, 1, 0)]
        for hh in range(2):
            s = _dot_nt(lhs, rhs_ref[hh]) + bias
            s_ref[hh] = s
            mb_ref[hh] = jnp.max(s, axis=0, keepdims=True)

    def consume(j, s_ref, mb_ref):
        start = pl.multiple_of(j * tk, tk)
        for hh in range(2):
            m_old = m_ref[hh]
            m_new = jnp.maximum(m_old, mb_ref[hh])
            p = jnp.exp2(s_ref[hh] - m_new).astype(BF16)
            vt = jnp.concatenate(
                [vt_ref[0, hh * HEAD_DIM:(hh + 1) * HEAD_DIM, pl.ds(start, tk)], ones], axis=0)
            acc_ref[hh] = acc_ref[hh] * jnp.exp2(m_old - m_new) + _dot(vt, p)
            m_ref[hh] = m_new

    scores(0, s_a_ref, mb_a_ref)

    def body(t, carry):
        j = 2 * t
        scores(j + 1, s_b_ref, mb_b_ref)
        consume(j, s_a_ref, mb_a_ref)
        scores(j + 2, s_a_ref, mb_a_ref)
        consume(j + 1, s_b_ref, mb_b_ref)
        return carry

    lax.fori_loop(0, i // 2, body, 0)

    @pl.when(i % 2 == 0)
    def _():
        consume(i, s_a_ref, mb_a_ref)

    @pl.when(i % 2 == 1)
    def _():
        scores(i, s_b_ref, mb_b_ref)
        consume(i - 1, s_a_ref, mb_a_ref)
        consume(i, s_b_ref, mb_b_ref)

    outs = []
    for hh in range(2):
        acc = acc_ref[hh]
        outs.append(acc[0:HEAD_DIM] * (1.0 / acc[HEAD_DIM:HEAD_DIM + 1]))
    o_ref[0] = jnp.concatenate(outs, axis=0).T.astype(o_ref.dtype)


def _attention(q, k, vt, kb, tq):
    B, S, _ = q.shape
    n_pairs = N_ATTN_HEADS // 2
    return pl.pallas_call(
        _attn_kernel,
        grid=(B, n_pairs, S // tq),
        in_specs=[pl.BlockSpec((1, tq, LANES), lambda b, p, i: (b, i, p)),
                  pl.BlockSpec((1, S, LANES), lambda b, p, i: (b, 0, p)),
                  pl.BlockSpec((1, LANES, S), lambda b, p, i: (b, p, 0)),
                  pl.BlockSpec((1, S, LANES), lambda b, p, i: (b, 0, 0))],
        out_specs=pl.BlockSpec((1, tq, LANES), lambda b, p, i: (b, i, p)),
        out_shape=jax.ShapeDtypeStruct((B, S, D_ATTN), BF16),
        scratch_shapes=[pltpu.VMEM((2, tq, 2 * LANES), BF16),
                        pltpu.VMEM((2, 1, tq), F32),
                        pltpu.VMEM((2, V_ROWS, tq), F32),
                        pltpu.VMEM((2, tq, tq), F32),
                        pltpu.VMEM((2, tq, tq), F32),
                        pltpu.VMEM((2, 1, tq), F32),
                        pltpu.VMEM((2, 1, tq), F32),
                        pltpu.VMEM((2, tq, tq), F32)],
        compiler_params=pltpu.CompilerParams(
            dimension_semantics=("arbitrary", "arbitrary", "arbitrary"),
            vmem_limit_bytes=VMEM_LIMIT_BYTES),
        name="fox_attention",
    )(q, k, vt, kb)


def _rglru_kernel(xr_ref, gr_ref, cw_ref, cb_ref, wbd_ref, ba_ref, bx_ref, lam_ref,
                  y_ref, xtail_ref, h_ref):
    ts = xr_ref.shape[1]
    width = xr_ref.shape[2]
    i = pl.program_id(1)

    @pl.when(i == 0)
    def _():
        xtail_ref[...] = jnp.zeros_like(xtail_ref)
        h_ref[...] = jnp.zeros_like(h_ref)

    n_groups = ts // 8
    x = xr_ref[0]
    x3 = x.reshape(n_groups, 8, width)
    tail = xtail_ref[...]
    sub = lax.broadcasted_iota(jnp.int32, (n_groups, 8, width), 1)
    xc3 = cb_ref[...] + cw_ref[RNN_CONV_W - 1:RNN_CONV_W, :] * x3
    for shift in range(1, RNN_CONV_W):
        kk = RNN_CONV_W - 1 - shift
        rot = pltpu.roll(x3, shift, axis=1)
        prev = jnp.concatenate([pltpu.roll(tail, shift, axis=0)[None], rot[:-1]], axis=0)
        xc3 = xc3 + cw_ref[kk:kk + 1, :] * jnp.where(sub < shift, prev, rot)
    xtail_ref[...] = x[ts - 8:ts]
    xc = xc3.reshape(ts, width)

    xcb = xc.astype(BF16)
    r_parts, i_parts = [], []
    for g in range(width // MXU_DIM):
        z = _dot(xcb[:, g * MXU_DIM:(g + 1) * MXU_DIM], wbd_ref[g])
        r_parts.append(z[:, 0:MXU_DIM])
        i_parts.append(z[:, MXU_DIM:2 * MXU_DIM])
    r = jax.nn.sigmoid(jnp.concatenate(r_parts, axis=1) + ba_ref[...])
    ig = jax.nn.sigmoid(jnp.concatenate(i_parts, axis=1) + bx_ref[...])
    log_a = (-RG_C) * r * _softplus(-lam_ref[...])
    a = jnp.exp(log_a)
    m2 = jnp.tanh(-log_a) * (1.0 + a * a)
    mult = jnp.where(m2 > 0.0, m2 * lax.rsqrt(m2), 0.0)
    row = lax.broadcasted_iota(jnp.int32, (ts, width), 0)
    mult = jnp.where(row == jnp.where(i == 0, 0, -1), 1.0, mult)
    b = mult * ig * xc

    a3 = a.reshape(n_groups, 8, width)
    b3 = b.reshape(n_groups, 8, width)
    for d in (1, 2, 4):
        valid = sub >= d
        b3 = jnp.where(valid, a3 * pltpu.roll(b3, d, axis=1), 0.0) + b3
        a3 = jnp.where(valid, a3 * pltpu.roll(a3, d, axis=1), a3)
    h_prev = h_ref[...]
    groups = []
    for g in range(n_groups):
        h_g = a3[g] * h_prev + b3[g]
        groups.append(h_g)
        h_prev = h_g[7:8]
    h_ref[...] = h_prev
    h = jnp.concatenate(groups, axis=0)
    y_ref[0] = (h * _gelu_tanh(gr_ref[0])).astype(y_ref.dtype)


def _rglru(xr, gr, cw, cb, wbd, ba, bx, lam, ts):
    B, S, W = xr.shape
    const = lambda shape: pl.BlockSpec(shape, lambda b, i: (0,) * len(shape))
    row = pl.BlockSpec((1, ts, W), lambda b, i: (b, i, 0))
    return pl.pallas_call(
        _rglru_kernel,
        grid=(B, S // ts),
        in_specs=[row, row, const(cw.shape), const((1, W)), const(wbd.shape),
                  const((1, W)), const((1, W)), const((1, W))],
        out_specs=row,
        out_shape=jax.ShapeDtypeStruct((B, S, W), BF16),
        scratch_shapes=[pltpu.VMEM((8, W), F32), pltpu.VMEM((1, W), F32)],
        compiler_params=pltpu.CompilerParams(
            dimension_semantics=("arbitrary", "arbitrary"), vmem_limit_bytes=VMEM_LIMIT_BYTES),
        name="rg_lru",
    )(xr, gr, cw, cb, wbd, ba, bx, lam)


def _outproj_kernel(ya_ref, yr_ref, x_ref, na_ref, nr_ref, wo_ref, npost_ref, nffn_ref,
                    x1_ref, h2_ref):
    da = ya_ref.shape[1]
    ya = _rms(ya_ref[...].astype(F32), na_ref[...]).astype(BF16)
    yr = _rms(yr_ref[...].astype(F32), nr_ref[...]).astype(BF16)
    y = _dot(ya, wo_ref[0:da, :]) + _dot(yr, wo_ref[da:, :])
    x1 = x_ref[...] + _rms(y, npost_ref[...])
    x1_ref[...] = x1
    h2_ref[...] = _rms(x1, nffn_ref[...]).astype(BF16)


def _out_projection(ya, yr, x, na, nr, wo, npost, nffn, tm):
    M, D = x.shape
    da, dr = ya.shape[1], yr.shape[1]
    const = lambda shape: pl.BlockSpec(shape, lambda i: (0,) * len(shape),
                                       pipeline_mode=pl.Buffered(1))
    row = lambda width: pl.BlockSpec((tm, width), lambda i: (i, 0))
    return pl.pallas_call(
        _outproj_kernel,
        grid=(M // tm,),
        in_specs=[row(da), row(dr), row(D), const((1, da)), const((1, dr)), const(wo.shape),
                  const((1, D)), const((1, D))],
        out_specs=[row(D), row(D)],
        out_shape=[jax.ShapeDtypeStruct((M, D), F32), jax.ShapeDtypeStruct((M, D), BF16)],
        compiler_params=pltpu.CompilerParams(
            dimension_semantics=("arbitrary",), vmem_limit_bytes=VMEM_LIMIT_BYTES),
        name="out_projection",
    )(ya, yr, x, na, nr, wo, npost, nffn)


def _ffn_kernel(h_ref, wg_ref, wu_ref, cw_ref, cb_ref, wd_ref, x1_ref, gain_ref,
                o_ref, acc_ref, gbuf_ref, gtail_ref, *, tiles_per_seq):
    tm = h_ref.shape[0]
    i = pl.program_id(0)
    c = pl.program_id(1)

    @pl.when(c == 0)
    def _():
        acc_ref[...] = jnp.zeros_like(acc_ref)

    h = h_ref[...]
    g = _dot(h, wg_ref[...])
    u = _dot(h, wu_ref[...])
    seq_start = (i % tiles_per_seq) == 0
    gbuf_ref[0:8, :] = jnp.where(seq_start, 0.0, gtail_ref[c])
    gbuf_ref[8:8 + tm, :] = g
    gtail_ref[c] = g[tm - 8:tm, :]
    conv = cb_ref[...] + cw_ref[FFN_CONV_W - 1:FFN_CONV_W, :] * g
    for kk in range(FFN_CONV_W - 1):
        conv = conv + cw_ref[kk:kk + 1, :] * gbuf_ref[pl.ds(8 - (FFN_CONV_W - 1) + kk, tm), :]
    act = (_gelu_tanh(conv) * u).astype(BF16)
    acc_ref[...] += _dot(act, wd_ref[...])

    @pl.when(c == pl.num_programs(1) - 1)
    def _():
        o_ref[...] = x1_ref[...] + _rms(acc_ref[...], gain_ref[...])


def _ffn(h2, wg, wu, cw, cb, wd, x1, gain, tm, tc, tiles_per_seq):
    M, D = h2.shape
    dff = wg.shape[1]
    nc = dff // tc
    return pl.pallas_call(
        functools.partial(_ffn_kernel, tiles_per_seq=tiles_per_seq),
        grid=(M // tm, nc),
        in_specs=[pl.BlockSpec((tm, D), lambda i, c: (i, 0)),
                  pl.BlockSpec((D, tc), lambda i, c: (0, c)),
                  pl.BlockSpec((D, tc), lambda i, c: (0, c)),
                  pl.BlockSpec((FFN_CONV_W, tc), lambda i, c: (0, c)),
                  pl.BlockSpec((1, tc), lambda i, c: (0, c)),
                  pl.BlockSpec((tc, D), lambda i, c: (c, 0)),
                  pl.BlockSpec((tm, D), lambda i, c: (i, 0)),
                  pl.BlockSpec((1, D), lambda i, c: (0, 0))],
        out_specs=pl.BlockSpec((tm, D), lambda i, c: (i, 0)),
        out_shape=jax.ShapeDtypeStruct((M, D), F32),
        scratch_shapes=[pltpu.VMEM((tm, D), F32),
                        pltpu.VMEM((tm + 8, tc), F32),
                        pltpu.VMEM((nc, 8, tc), F32)],
        compiler_params=pltpu.CompilerParams(
            dimension_semantics=("arbitrary", "arbitrary"), vmem_limit_bytes=VMEM_LIMIT_BYTES),
        name="conv_gated_mlp",
    )(h2, wg, wu, cw, cb, wd, x1, gain)


def _block_diag(w, group):
    n, r, _ = w.shape
    w = w.reshape(n // group, group, r, r)
    eye = jnp.eye(group, dtype=w.dtype)
    return jnp.einsum("gaij,ab->gaibj", w, eye).reshape(n // group, group * r, group * r)


def _layer(x, norm_mix_pre, w_in, b_forget, conv_rnn_w, conv_rnn_b, w_rg_a, b_rg_a, w_rg_x,
           b_rg_x, rg_lambda, norm_attn_out, norm_rnn_out, w_out, norm_mix_post, norm_ffn_pre,
           w_gate, w_up, conv_ffn_w, conv_ffn_b, w_down, norm_ffn_post):
    B, S, D = x.shape
    d_rnn = conv_rnn_w.shape[1]
    t_seq = min(512, S)

    nh = N_ATTN_HEADS
    w_q, w_k, w_v = w_in[:, 0:D_ATTN], w_in[:, D_ATTN:2 * D_ATTN], w_in[:, 2 * D_ATTN:3 * D_ATTN]
    w_f = w_in[:, 3 * D_ATTN:3 * D_ATTN + nh]
    w_xg = w_in[:, 3 * D_ATTN + nh:]
    w_f_rep = jnp.concatenate([w_f, w_f, w_f, jnp.zeros((D, LANES - 3 * nh), w_in.dtype)], axis=1)
    wcat = jnp.concatenate([w_q, w_k, w_xg, w_f_rep], axis=1).astype(BF16)
    wvt = w_v.T.astype(BF16)
    bf_rep = jnp.concatenate([b_forget, b_forget, b_forget,
                              jnp.zeros((LANES - 3 * nh,), b_forget.dtype)]).reshape(1, LANES)
    group = MXU_DIM // RNN_BLOCK
    wbd = jnp.concatenate([_block_diag(w_rg_a, group), _block_diag(w_rg_x, group)],
                          axis=2).astype(BF16)
    row = lambda v: v.reshape(1, -1)

    q, k, vt, kb, xr, gr = _in_projection(x, row(norm_mix_pre), wcat, wvt, bf_rep, t_seq)
    y_attn = _attention(q, k, vt, kb, t_seq)
    y_rnn = _rglru(xr, gr, conv_rnn_w, row(conv_rnn_b), wbd, row(b_rg_a), row(b_rg_x),
                   row(rg_lambda), t_seq)
    M = B * S
    x1, h2 = _out_projection(y_attn.reshape(M, D_ATTN), y_rnn.reshape(M, d_rnn), x.reshape(M, D),
                             row(norm_attn_out), row(norm_rnn_out), w_out.astype(BF16),
                             row(norm_mix_post), row(norm_ffn_pre), t_seq)
    out = _ffn(h2, w_gate.astype(BF16), w_up.astype(BF16), conv_ffn_w, row(conv_ffn_b),
               w_down.astype(BF16), x1, row(norm_ffn_post), t_seq, 512, S // t_seq)
    return out.reshape(B, S, D)


def kernel(x, norm_mix_pre, w_in, b_forget, conv_rnn_w, conv_rnn_b, w_rg_a, b_rg_a, w_rg_x, b_rg_x, rg_lambda, norm_attn_out, norm_rnn_out, w_out, norm_mix_post, norm_ffn_pre, w_gate, w_up, conv_ffn_w, conv_ffn_b, w_down, norm_ffn_post):
    params = (norm_mix_pre, w_in, b_forget, conv_rnn_w, conv_rnn_b, w_rg_a, b_rg_a, w_rg_x, b_rg_x,
              rg_lambda, norm_attn_out, norm_rnn_out, w_out, norm_mix_post, norm_ffn_pre, w_gate,
              w_up, conv_ffn_w, conv_ffn_b, w_down, norm_ffn_post)
    for l in range(norm_mix_pre.shape[0]):
        x = _layer(x, *(p[l] for p in params))
    return x
```

```python
import functools
import math

import jax
import jax.numpy as jnp
from jax import lax
from jax.experimental import pallas as pl
from jax.experimental.pallas import tpu as pltpu

N_ATTN_HEADS = 16
HEAD_DIM = 64
D_ATTN = N_ATTN_HEADS * HEAD_DIM
N_RNN_BLOCKS = 16
RNN_BLOCK = 64
RNN_CONV_W = 4
RG_C = 8.0
FFN_CONV_W = 3
EPS = 1e-6
LOG2E = 1.4426950408889634

LANES = 128
MXU_DIM = 256
VMEM_LIMIT_BYTES = 56 * 1024 * 1024

F32 = jnp.float32
BF16 = jnp.bfloat16


def _rms(x, gain):
    return x * lax.rsqrt(jnp.mean(x * x, axis=-1, keepdims=True) + EPS) * gain


def _gelu_tanh(x):
    cdf = 0.5 * (1.0 + jnp.tanh(math.sqrt(2.0 / math.pi) * (x + 0.044715 * (x * x * x))))
    return x * cdf


def _softplus(x):
    return jnp.maximum(x, 0.0) + jnp.log1p(jnp.exp(-jnp.abs(x)))


def _split3(x):
    hi = x.astype(BF16).astype(F32)
    r = x - hi
    mid = r.astype(BF16).astype(F32)
    lo = (r - mid).astype(BF16).astype(F32)
    return hi, mid, lo


def _dot(a, b):
    return jnp.dot(a, b, preferred_element_type=F32)


def _dot_nt(a, b):
    return lax.dot_general(a, b, (((1,), (1,)), ((), ())), preferred_element_type=F32)


def _proj_kernel(x_ref, g_ref, wcat_ref, wqvt_ref, bf_ref,
                 qt_ref, k_ref, vt_ref, kb_ref, xr_ref, gr_ref, carry_ref):
    tm = x_ref.shape[1]

    @pl.when(pl.program_id(1) == 0)
    def _():
        carry_ref[...] = jnp.zeros_like(carry_ref)

    h = _rms(x_ref[0], g_ref[...]).astype(BF16)
    qt_ref[0] = (_dot_nt(wqvt_ref[0:D_ATTN, :], h) * (HEAD_DIM ** -0.5 * LOG2E)).astype(BF16)
    vt_ref[0] = _dot_nt(wqvt_ref[D_ATTN:2 * D_ATTN, :], h).astype(BF16)
    k_ref[0] = _dot(h, wcat_ref[:, 0:D_ATTN]).astype(BF16)
    xr_ref[0] = _dot(h, wcat_ref[:, D_ATTN:2 * D_ATTN])
    gr_ref[0] = _dot(h, wcat_ref[:, 2 * D_ATTN:3 * D_ATTN])

    fl = _dot(h, wcat_ref[:, 3 * D_ATTN:3 * D_ATTN + LANES]) + bf_ref[...]
    logf = jnp.minimum(fl, 0.0) - jnp.log1p(jnp.exp(-jnp.abs(fl)))
    lane = lax.broadcasted_iota(jnp.int32, (tm, LANES), 1)
    logf = jnp.where(lane < 3 * N_ATTN_HEADS, logf, 0.0)
    hi, mid, lo = _split3(logf)
    pieces = jnp.concatenate([hi, mid, lo], axis=1).astype(BF16)
    rows = lax.broadcasted_iota(jnp.int32, (tm, tm), 0)
    cols = lax.broadcasted_iota(jnp.int32, (tm, tm), 1)
    tri = jnp.where(rows >= cols, 1.0, 0.0).astype(BF16)
    cs = _dot(tri, pieces)
    c = cs[:, 0:LANES] + cs[:, LANES:2 * LANES] + cs[:, 2 * LANES:3 * LANES] + carry_ref[...]
    carry_ref[...] = c[tm - 1:tm, :]
    bhi, bmid, blo = _split3(c * (-LOG2E))
    kb = jnp.where(lane < N_ATTN_HEADS, bhi, jnp.where(lane < 2 * N_ATTN_HEADS, bmid, blo))
    kb_ref[0] = kb.astype(BF16)


def _in_projection(x, gain, wcat, wqvt, bf_rep, tm):
    B, S, D = x.shape
    nq = wcat.shape[1]
    const = lambda shape: pl.BlockSpec(shape, lambda b, i: (0,) * len(shape),
                                       pipeline_mode=pl.Buffered(1))
    row = lambda width: pl.BlockSpec((1, tm, width), lambda b, i: (b, i, 0))
    col = pl.BlockSpec((1, D_ATTN, tm), lambda b, i: (b, 0, i))
    return pl.pallas_call(
        _proj_kernel,
        grid=(B, S // tm),
        in_specs=[row(D), const((1, D)), const((D, nq)), const((2 * D_ATTN, D)), const((1, LANES))],
        out_specs=[col, row(D_ATTN), col, row(LANES), row(D_ATTN), row(D_ATTN)],
        out_shape=[jax.ShapeDtypeStruct((B, D_ATTN, S), BF16),
                   jax.ShapeDtypeStruct((B, S, D_ATTN), BF16),
                   jax.ShapeDtypeStruct((B, D_ATTN, S), BF16),
                   jax.ShapeDtypeStruct((B, S, LANES), BF16),
                   jax.ShapeDtypeStruct((B, S, D_ATTN), F32),
                   jax.ShapeDtypeStruct((B, S, D_ATTN), F32)],
        scratch_shapes=[pltpu.VMEM((1, LANES), F32)],
        compiler_params=pltpu.CompilerParams(
            dimension_semantics=("arbitrary", "arbitrary"), vmem_limit_bytes=VMEM_LIMIT_BYTES),
        name="in_projection",
    )(x, gain, wcat, wqvt, bf_rep)


V_ROWS = HEAD_DIM + 16


def _attn_kernel(qt_ref, k_ref, vt_ref, kb_ref, o_ref,
                 rhs_ref, m_ref, acc_ref, s_a_ref, s_b_ref, mb_a_ref, mb_b_ref):
    tq = qt_ref.shape[2]
    tk = tq
    hp = pl.program_id(1)
    i = pl.program_id(2)

    qt = qt_ref[0]
    row = lax.broadcasted_iota(jnp.int32, (LANES, tq), 0)
    for hh in range(2):
        head = 2 * hp + hh
        qh = jnp.where((row >= hh * HEAD_DIM) & (row < (hh + 1) * HEAD_DIM), qt, jnp.zeros_like(qt))
        sel = (row == head) | (row == head + N_ATTN_HEADS) | (row == head + 2 * N_ATTN_HEADS)
        rhs_ref[hh, 0:LANES, :] = qh
        rhs_ref[hh, LANES:2 * LANES, :] = jnp.where(sel, 1.0, 0.0).astype(BF16)
    m_ref[...] = jnp.full_like(m_ref, -jnp.inf)
    acc_ref[...] = jnp.zeros_like(acc_ref)
    ones = jnp.ones((V_ROWS - HEAD_DIM, tk), BF16)

    def scores(j, s_ref, mb_ref, diagonal):
        start = pl.multiple_of(j * tk, tk)
        lhs = jnp.concatenate([k_ref[0, pl.ds(start, tk), :], kb_ref[0, pl.ds(start, tk), :]], axis=1)
        for hh in range(2):
            s = _dot(lhs, rhs_ref[hh])
            if diagonal:
                key = lax.broadcasted_iota(jnp.int32, (tk, tq), 0)
                qry = lax.broadcasted_iota(jnp.int32, (tk, tq), 1)
                s = jnp.where(key <= qry, s, -jnp.inf)
            s_ref[hh] = s
            mb_ref[hh] = jnp.max(s, axis=0, keepdims=True)

    def consume(j, s_ref, mb_ref):
        start = pl.multiple_of(j * tk, tk)
        for hh in range(2):
            m_old = m_ref[hh]
            m_new = jnp.maximum(m_old, mb_ref[hh])
            p = jnp.exp2(s_ref[hh] - m_new).astype(BF16)
            vt = jnp.concatenate(
                [vt_ref[0, hh * HEAD_DIM:(hh + 1) * HEAD_DIM, pl.ds(start, tk)], ones], axis=0)
            acc_ref[hh] = acc_ref[hh] * jnp.exp2(m_old - m_new) + _dot(vt, p)
            m_ref[hh] = m_new

    slot_a = (s_a_ref, mb_a_ref)
    slot_b = (s_b_ref, mb_b_ref)

    @pl.when(i == 0)
    def _():
        scores(0, *slot_a, diagonal=True)
        consume(0, *slot_a)

    @pl.when(i > 0)
    def _():
        scores(0, *slot_a, diagonal=False)
        n_loop = (i - 1) // 2

        def body(t, carry):
            j = 2 * t
            scores(j + 1, *slot_b, diagonal=False)
            consume(j, *slot_a)
            scores(j + 2, *slot_a, diagonal=False)
            consume(j + 1, *slot_b)
            return carry

        lax.fori_loop(0, n_loop, body, 0)
        j = 2 * n_loop

        @pl.when(i - j == 1)
        def _():
            scores(i, *slot_b, diagonal=True)
            consume(j, *slot_a)
            consume(i, *slot_b)

        @pl.when(i - j == 2)
        def _():
            scores(j + 1, *slot_b, diagonal=False)
            consume(j, *slot_a)
            scores(i, *slot_a, diagonal=True)
            consume(j + 1, *slot_b)
            consume(i, *slot_a)

    outs = []
    for hh in range(2):
        acc = acc_ref[hh]
        outs.append(acc[0:HEAD_DIM] * (1.0 / acc[HEAD_DIM:HEAD_DIM + 1]))
    o_ref[0] = jnp.concatenate(outs, axis=0).T.astype(o_ref.dtype)


def _attention(qt, k, vt, kb, tq):
    B, S, _ = k.shape
    n_pairs = N_ATTN_HEADS // 2
    return pl.pallas_call(
        _attn_kernel,
        grid=(B, n_pairs, S // tq),
        in_specs=[pl.BlockSpec((1, LANES, tq), lambda b, p, i: (b, p, i)),
                  pl.BlockSpec((1, S, LANES), lambda b, p, i: (b, 0, p)),
                  pl.BlockSpec((1, LANES, S), lambda b, p, i: (b, p, 0)),
                  pl.BlockSpec((1, S, LANES), lambda b, p, i: (b, 0, 0))],
        out_specs=pl.BlockSpec((1, tq, LANES), lambda b, p, i: (b, i, p)),
        out_shape=jax.ShapeDtypeStruct((B, S, D_ATTN), BF16),
        scratch_shapes=[pltpu.VMEM((2, 2 * LANES, tq), BF16),
                        pltpu.VMEM((2, 1, tq), F32),
                        pltpu.VMEM((2, V_ROWS, tq), F32),
                        pltpu.VMEM((2, tq, tq), F32),
                        pltpu.VMEM((2, tq, tq), F32),
                        pltpu.VMEM((2, 1, tq), F32),
                        pltpu.VMEM((2, 1, tq), F32)],
        compiler_params=pltpu.CompilerParams(
            dimension_semantics=("arbitrary", "arbitrary", "arbitrary"),
            vmem_limit_bytes=VMEM_LIMIT_BYTES),
        name="fox_attention",
    )(qt, k, vt, kb)


def _rglru_kernel(xr_ref, gr_ref, cw_ref, cb_ref, wbd_ref, ba_ref, bx_ref, lam_ref,
                  y_ref, xtail_ref, h_ref):
    ts = xr_ref.shape[1]
    width = xr_ref.shape[2]
    i = pl.program_id(1)

    @pl.when(i == 0)
    def _():
        xtail_ref[...] = jnp.zeros_like(xtail_ref)
        h_ref[...] = jnp.zeros_like(h_ref)

    n_groups = ts // 8
    x = xr_ref[0]
    x3 = x.reshape(n_groups, 8, width)
    tail = xtail_ref[...]
    sub = lax.broadcasted_iota(jnp.int32, (n_groups, 8, width), 1)
    xc3 = cb_ref[...] + cw_ref[RNN_CONV_W - 1:RNN_CONV_W, :] * x3
    for shift in range(1, RNN_CONV_W):
        kk = RNN_CONV_W - 1 - shift
        rot = pltpu.roll(x3, shift, axis=1)
        prev = jnp.concatenate([pltpu.roll(tail, shift, axis=0)[None], rot[:-1]], axis=0)
        xc3 = xc3 + cw_ref[kk:kk + 1, :] * jnp.where(sub < shift, prev, rot)
    xtail_ref[...] = x[ts - 8:ts]
    xc = xc3.reshape(ts, width)

    xcb = xc.astype(BF16)
    r_parts, i_parts = [], []
    for g in range(width // MXU_DIM):
        z = _dot(xcb[:, g * MXU_DIM:(g + 1) * MXU_DIM], wbd_ref[g])
        r_parts.append(z[:, 0:MXU_DIM])
        i_parts.append(z[:, MXU_DIM:2 * MXU_DIM])
    r = jax.nn.sigmoid(jnp.concatenate(r_parts, axis=1) + ba_ref[...])
    ig = jax.nn.sigmoid(jnp.concatenate(i_parts, axis=1) + bx_ref[...])
    log_a = (-RG_C) * r * _softplus(-lam_ref[...])
    a = jnp.exp(log_a)
    m2 = jnp.tanh(-log_a) * (1.0 + a * a)
    mult = jnp.where(m2 > 0.0, m2 * lax.rsqrt(m2), 0.0)
    row = lax.broadcasted_iota(jnp.int32, (ts, width), 0)
    mult = jnp.where(row == jnp.where(i == 0, 0, -1), 1.0, mult)
    b = mult * ig * xc

    a3 = a.reshape(n_groups, 8, width)
    b3 = b.reshape(n_groups, 8, width)
    for d in (1, 2, 4):
        valid = sub >= d
        b3 = jnp.where(valid, a3 * pltpu.roll(b3, d, axis=1), 0.0) + b3
        a3 = jnp.where(valid, a3 * pltpu.roll(a3, d, axis=1), a3)
    h_prev = h_ref[...]
    groups = []
    for g in range(n_groups):
        h_g = a3[g] * h_prev + b3[g]
        groups.append(h_g)
        h_prev = h_g[7:8]
    h_ref[...] = h_prev
    h = jnp.concatenate(groups, axis=0)
    y_ref[0] = (h * _gelu_tanh(gr_ref[0])).astype(y_ref.dtype)


def _rglru(xr, gr, cw, cb, wbd, ba, bx, lam, ts):
    B, S, W = xr.shape
    const = lambda shape: pl.BlockSpec(shape, lambda b, i: (0,) * len(shape))
    row = pl.BlockSpec((1, ts, W), lambda b, i: (b, i, 0))
    return pl.pallas_call(
        _rglru_kernel,
        grid=(B, S // ts),
        in_specs=[row, row, const(cw.shape), const((1, W)), const(wbd.shape),
                  const((1, W)), const((1, W)), const((1, W))],
        out_specs=row,
        out_shape=jax.ShapeDtypeStruct((B, S, W), BF16),
        scratch_shapes=[pltpu.VMEM((8, W), F32), pltpu.VMEM((1, W), F32)],
        compiler_params=pltpu.CompilerParams(
            dimension_semantics=("arbitrary", "arbitrary"), vmem_limit_bytes=VMEM_LIMIT_BYTES),
        name="rg_lru",
    )(xr, gr, cw, cb, wbd, ba, bx, lam)


def _outproj_kernel(ya_ref, yr_ref, x_ref, na_ref, nr_ref, wo_ref, npost_ref, nffn_ref,
                    x1_ref, h2_ref):
    da = ya_ref.shape[1]
    ya = _rms(ya_ref[...].astype(F32), na_ref[...]).astype(BF16)
    yr = _rms(yr_ref[...].astype(F32), nr_ref[...]).astype(BF16)
    y = _dot(ya, wo_ref[0:da, :]) + _dot(yr, wo_ref[da:, :])
    x1 = x_ref[...] + _rms(y, npost_ref[...])
    x1_ref[...] = x1
    h2_ref[...] = _rms(x1, nffn_ref[...]).astype(BF16)


def _out_projection(ya, yr, x, na, nr, wo, npost, nffn, tm):
    M, D = x.shape
    da, dr = ya.shape[1], yr.shape[1]
    const = lambda shape: pl.BlockSpec(shape, lambda i: (0,) * len(shape),
                                       pipeline_mode=pl.Buffered(1))
    row = lambda width: pl.BlockSpec((tm, width), lambda i: (i, 0))
    return pl.pallas_call(
        _outproj_kernel,
        grid=(M // tm,),
        in_specs=[row(da), row(dr), row(D), const((1, da)), const((1, dr)), const(wo.shape),
                  const((1, D)), const((1, D))],
        out_specs=[row(D), row(D)],
        out_shape=[jax.ShapeDtypeStruct((M, D), F32), jax.ShapeDtypeStruct((M, D), BF16)],
        compiler_params=pltpu.CompilerParams(
            dimension_semantics=("arbitrary",), vmem_limit_bytes=VMEM_LIMIT_BYTES),
        name="out_projection",
    )(ya, yr, x, na, nr, wo, npost, nffn)


def _ffn_kernel(h_ref, wg_ref, wu_ref, cw_ref, cb_ref, wd_ref, x1_ref, gain_ref,
                o_ref, acc_ref, gtail_ref, *, tiles_per_seq):
    tm = h_ref.shape[0]
    tc = wg_ref.shape[1]
    i = pl.program_id(0)
    c = pl.program_id(1)

    @pl.when(c == 0)
    def _():
        acc_ref[...] = jnp.zeros_like(acc_ref)

    h = h_ref[...]
    g = _dot(h, wg_ref[...])
    u = _dot(h, wu_ref[...])
    n_groups = tm // 8
    seq_start = (i % tiles_per_seq) == 0
    tail = jnp.where(seq_start, 0.0, gtail_ref[c])
    gtail_ref[c] = g[tm - 8:tm, :]
    g3 = g.reshape(n_groups, 8, tc)
    sub = lax.broadcasted_iota(jnp.int32, (n_groups, 8, tc), 1)
    conv = cb_ref[...] + cw_ref[FFN_CONV_W - 1:FFN_CONV_W, :] * g3
    for shift in range(1, FFN_CONV_W):
        kk = FFN_CONV_W - 1 - shift
        rot = pltpu.roll(g3, shift, axis=1)
        prev = jnp.concatenate([pltpu.roll(tail, shift, axis=0)[None], rot[:-1]], axis=0)
        conv = conv + cw_ref[kk:kk + 1, :] * jnp.where(sub < shift, prev, rot)
    act = (_gelu_tanh(conv.reshape(tm, tc)) * u).astype(BF16)
    acc_ref[...] += _dot(act, wd_ref[...])

    @pl.when(c == pl.num_programs(1) - 1)
    def _():
        o_ref[...] = x1_ref[...] + _rms(acc_ref[...], gain_ref[...])


def _ffn(h2, wg, wu, cw, cb, wd, x1, gain, tm, tc, tiles_per_seq):
    M, D = h2.shape
    dff = wg.shape[1]
    nc = dff // tc
    return pl.pallas_call(
        functools.partial(_ffn_kernel, tiles_per_seq=tiles_per_seq),
        grid=(M // tm, nc),
        in_specs=[pl.BlockSpec((tm, D), lambda i, c: (i, 0)),
                  pl.BlockSpec((D, tc), lambda i, c: (0, c)),
                  pl.BlockSpec((D, tc), lambda i, c: (0, c)),
                  pl.BlockSpec((FFN_CONV_W, tc), lambda i, c: (0, c)),
                  pl.BlockSpec((1, tc), lambda i, c: (0, c)),
                  pl.BlockSpec((tc, D), lambda i, c: (c, 0)),
                  pl.BlockSpec((tm, D), lambda i, c: (i, 0)),
                  pl.BlockSpec((1, D), lambda i, c: (0, 0))],
        out_specs=pl.BlockSpec((tm, D), lambda i, c: (i, 0)),
        out_shape=jax.ShapeDtypeStruct((M, D), F32),
        scratch_shapes=[pltpu.VMEM((tm, D), F32),
                        pltpu.VMEM((nc, 8, tc), F32)],
        compiler_params=pltpu.CompilerParams(
            dimension_semantics=("arbitrary", "arbitrary"), vmem_limit_bytes=VMEM_LIMIT_BYTES),
        name="conv_gated_mlp",
    )(h2, wg, wu, cw, cb, wd, x1, gain)


def _block_diag(w, group):
    n, r, _ = w.shape
    w = w.reshape(n // group, group, r, r)
    eye = jnp.eye(group, dtype=w.dtype)
    return jnp.einsum("gaij,ab->gaibj", w, eye).reshape(n // group, group * r, group * r)


def _layer(x, norm_mix_pre, w_in, b_forget, conv_rnn_w, conv_rnn_b, w_rg_a, b_rg_a, w_rg_x,
           b_rg_x, rg_lambda, norm_attn_out, norm_rnn_out, w_out, norm_mix_post, norm_ffn_pre,
           w_gate, w_up, conv_ffn_w, conv_ffn_b, w_down, norm_ffn_post):
    B, S, D = x.shape
    d_rnn = conv_rnn_w.shape[1]
    t_seq = min(512, S)

    nh = N_ATTN_HEADS
    w_q, w_k, w_v = w_in[:, 0:D_ATTN], w_in[:, D_ATTN:2 * D_ATTN], w_in[:, 2 * D_ATTN:3 * D_ATTN]
    w_f = w_in[:, 3 * D_ATTN:3 * D_ATTN + nh]
    w_xg = w_in[:, 3 * D_ATTN + nh:]
    w_f_rep = jnp.concatenate([w_f, w_f, w_f, jnp.zeros((D, LANES - 3 * nh), w_in.dtype)], axis=1)
    wcat = jnp.concatenate([w_k, w_xg, w_f_rep], axis=1).astype(BF16)
    wqvt = jnp.concatenate([w_q, w_v], axis=1).T.astype(BF16)
    bf_rep = jnp.concatenate([b_forget, b_forget, b_forget,
                              jnp.zeros((LANES - 3 * nh,), b_forget.dtype)]).reshape(1, LANES)
    group = MXU_DIM // RNN_BLOCK
    wbd = jnp.concatenate([_block_diag(w_rg_a, group), _block_diag(w_rg_x, group)],
                          axis=2).astype(BF16)
    row = lambda v: v.reshape(1, -1)

    qt, k, vt, kb, xr, gr = _in_projection(x, row(norm_mix_pre), wcat, wqvt, bf_rep, t_seq)
    y_attn = _attention(qt, k, vt, kb, t_seq)
    y_rnn = _rglru(xr, gr, conv_rnn_w, row(conv_rnn_b), wbd, row(b_rg_a), row(b_rg_x),
                   row(rg_lambda), t_seq)
    M = B * S
    x1, h2 = _out_projection(y_attn.reshape(M, D_ATTN), y_rnn.reshape(M, d_rnn), x.reshape(M, D),
                             row(norm_attn_out), row(norm_rnn_out), w_out.astype(BF16),
                             row(norm_mix_post), row(norm_ffn_pre), t_seq)
    out = _ffn(h2, w_gate.astype(BF16), w_up.astype(BF16), conv_ffn_w, row(conv_ffn_b),
               w_down.astype(BF16), x1, row(norm_ffn_post), t_seq, 512, S // t_seq)
    return out.reshape(B, S, D)


def kernel(x, norm_mix_pre, w_in, b_forget, conv_rnn_w, conv_rnn_b, w_rg_a, b_rg_a, w_rg_x, b_rg_x, rg_lambda, norm_attn_out, norm_rnn_out, w_out, norm_mix_post, norm_ffn_pre, w_gate, w_up, conv_ffn_w, conv_ffn_b, w_down, norm_ffn_post):
    params = (norm_mix_pre, w_in, b_forget, conv_rnn_w, conv_rnn_b, w_rg_a, b_rg_a, w_rg_x, b_rg_x,
              rg_lambda, norm_attn_out, norm_rnn_out, w_out, norm_mix_post, norm_ffn_pre, w_gate,
              w_up, conv_ffn_w, conv_ffn_b, w_down, norm_ffn_post)
    for l in range(norm_mix_pre.shape[0]):
        x = _layer(x, *(p[l] for p in params))
    return x
```

```python
import functools
import math

import jax
import jax.numpy as jnp
from jax import lax
from jax.experimental import pallas as pl
from jax.experimental.pallas import tpu as pltpu

N_ATTN_HEADS = 16
HEAD_DIM = 64
D_ATTN = N_ATTN_HEADS * HEAD_DIM
N_RNN_BLOCKS = 16
RNN_BLOCK = 64
RNN_CONV_W = 4
RG_C = 8.0
FFN_CONV_W = 3
EPS = 1e-6
LOG2E = 1.4426950408889634

LANES = 128
MXU_DIM = 256
VMEM_LIMIT_BYTES = 56 * 1024 * 1024

F32 = jnp.float32
BF16 = jnp.bfloat16


def _rms(x, gain):
    return x * lax.rsqrt(jnp.mean(x * x, axis=-1, keepdims=True) + EPS) * gain


def _gelu_tanh(x):
    cdf = 0.5 * (1.0 + jnp.tanh(math.sqrt(2.0 / math.pi) * (x + 0.044715 * (x * x * x))))
    return x * cdf


def _softplus(x):
    return jnp.maximum(x, 0.0) + jnp.log1p(jnp.exp(-jnp.abs(x)))


def _split3(x):
    hi = x.astype(BF16).astype(F32)
    r = x - hi
    mid = r.astype(BF16).astype(F32)
    lo = (r - mid).astype(BF16).astype(F32)
    return hi, mid, lo


def _dot(a, b):
    return jnp.dot(a, b, preferred_element_type=F32)


def _dot_nt(a, b):
    return lax.dot_general(a, b, (((1,), (1,)), ((), ())), preferred_element_type=F32)


def _proj_kernel(x_ref, g_ref, wqt_ref, wvt_ref, wk_ref, wxg_ref, wf_ref, bf_ref,
                 qt_ref, k_ref, vt_ref, kb_ref, xr_ref, gr_ref, carry_ref):
    tm = x_ref.shape[1]

    @pl.when(pl.program_id(1) == 0)
    def _():
        carry_ref[...] = jnp.zeros_like(carry_ref)

    h = _rms(x_ref[0], g_ref[...]).astype(BF16)
    lane = lax.broadcasted_iota(jnp.int32, (tm, LANES), 1)

    fl = _dot(h, wf_ref[...]) + bf_ref[...]
    logf = jnp.minimum(fl, 0.0) - jnp.log1p(jnp.exp(-jnp.abs(fl)))
    logf = jnp.where(lane < 3 * N_ATTN_HEADS, logf, 0.0)
    hi, mid, lo = _split3(logf)
    pieces = jnp.concatenate([hi, mid, lo], axis=1).astype(BF16)

    qt_ref[0] = (_dot_nt(wqt_ref[...], h) * (HEAD_DIM ** -0.5 * LOG2E)).astype(BF16)

    rows = lax.broadcasted_iota(jnp.int32, (tm, tm), 0)
    cols = lax.broadcasted_iota(jnp.int32, (tm, tm), 1)
    tri = jnp.where(rows >= cols, 1.0, 0.0).astype(BF16)
    cs = _dot(tri, pieces)
    c = cs[:, 0:LANES] + cs[:, LANES:2 * LANES] + cs[:, 2 * LANES:3 * LANES] + carry_ref[...]
    carry_ref[...] = c[tm - 1:tm, :]
    bhi, bmid, blo = _split3(c * (-LOG2E))
    kb = jnp.where(lane < N_ATTN_HEADS, bhi, jnp.where(lane < 2 * N_ATTN_HEADS, bmid, blo))
    kb_ref[0] = kb.astype(BF16)

    vt_ref[0] = _dot_nt(wvt_ref[...], h).astype(BF16)
    k_ref[0] = _dot(h, wk_ref[...]).astype(BF16)
    xr_ref[0] = _dot(h, wxg_ref[:, 0:D_ATTN])
    gr_ref[0] = _dot(h, wxg_ref[:, D_ATTN:2 * D_ATTN])


def _in_projection(x, gain, wqt, wvt, wk, wxg, wf, bf_rep, tm):
    B, S, D = x.shape
    const = lambda a: pl.BlockSpec(a.shape, lambda b, i: (0,) * a.ndim, pipeline_mode=pl.Buffered(1))
    row = lambda width: pl.BlockSpec((1, tm, width), lambda b, i: (b, i, 0))
    col = pl.BlockSpec((1, D_ATTN, tm), lambda b, i: (b, 0, i))
    weights = (gain, wqt, wvt, wk, wxg, wf, bf_rep)
    return pl.pallas_call(
        _proj_kernel,
        grid=(B, S // tm),
        in_specs=[row(D)] + [const(w) for w in weights],
        out_specs=[col, row(D_ATTN), col, row(LANES), row(D_ATTN), row(D_ATTN)],
        out_shape=[jax.ShapeDtypeStruct((B, D_ATTN, S), BF16),
                   jax.ShapeDtypeStruct((B, S, D_ATTN), BF16),
                   jax.ShapeDtypeStruct((B, D_ATTN, S), BF16),
                   jax.ShapeDtypeStruct((B, S, LANES), BF16),
                   jax.ShapeDtypeStruct((B, S, D_ATTN), F32),
                   jax.ShapeDtypeStruct((B, S, D_ATTN), F32)],
        scratch_shapes=[pltpu.VMEM((1, LANES), F32)],
        compiler_params=pltpu.CompilerParams(
            dimension_semantics=("arbitrary", "arbitrary"), vmem_limit_bytes=VMEM_LIMIT_BYTES),
        name="in_projection",
    )(x, *weights)


V_ROWS = HEAD_DIM + 16


def _attn_query_block(i, hp, qt, k_ref, vt_ref, kb_ref,
                      rhs_ref, m_ref, acc_ref, s_a_ref, s_b_ref, mb_a_ref, mb_b_ref):
    tq = qt.shape[1]
    tk = tq

    row = lax.broadcasted_iota(jnp.int32, (LANES, tq), 0)
    for hh in range(2):
        head = 2 * hp + hh
        qh = jnp.where((row >= hh * HEAD_DIM) & (row < (hh + 1) * HEAD_DIM), qt, jnp.zeros_like(qt))
        sel = (row == head) | (row == head + N_ATTN_HEADS) | (row == head + 2 * N_ATTN_HEADS)
        rhs_ref[hh, 0:LANES, :] = qh
        rhs_ref[hh, LANES:2 * LANES, :] = jnp.where(sel, 1.0, 0.0).astype(BF16)
    m_ref[...] = jnp.full_like(m_ref, -jnp.inf)
    acc_ref[...] = jnp.zeros_like(acc_ref)
    ones = jnp.ones((V_ROWS - HEAD_DIM, tk), BF16)

    def scores(j, slot, diagonal, hh):
        s_ref, mb_ref = slot
        start = pl.multiple_of(j * tk, tk)
        lhs = jnp.concatenate([k_ref[0, pl.ds(start, tk), :], kb_ref[0, pl.ds(start, tk), :]], axis=1)
        s = _dot(lhs, rhs_ref[hh])
        if diagonal:
            key = lax.broadcasted_iota(jnp.int32, (tk, tq), 0)
            qry = lax.broadcasted_iota(jnp.int32, (tk, tq), 1)
            s = jnp.where(key <= qry, s, -jnp.inf)
        s_ref[hh] = s
        mb_ref[hh] = jnp.max(s, axis=0, keepdims=True)

    def consume(j, slot, hh):
        s_ref, mb_ref = slot
        start = pl.multiple_of(j * tk, tk)
        m_old = m_ref[hh]
        m_new = jnp.maximum(m_old, mb_ref[hh])
        p = jnp.exp2(s_ref[hh] - m_new).astype(BF16)
        vt = jnp.concatenate(
            [vt_ref[0, hh * HEAD_DIM:(hh + 1) * HEAD_DIM, pl.ds(start, tk)], ones], axis=0)
        acc_ref[hh] = acc_ref[hh] * jnp.exp2(m_old - m_new) + _dot(vt, p)
        m_ref[hh] = m_new

    def stage(js, slot_s, diagonal, jc, slot_c):
        for hh in range(2):
            scores(js, slot_s, diagonal, hh)
            consume(jc, slot_c, hh)

    slot_a = (s_a_ref, mb_a_ref)
    slot_b = (s_b_ref, mb_b_ref)

    @pl.when(i == 0)
    def _():
        for hh in range(2):
            scores(0, slot_a, True, hh)
            consume(0, slot_a, hh)

    @pl.when(i > 0)
    def _():
        for hh in range(2):
            scores(0, slot_a, False, hh)
        n_loop = (i - 1) // 2

        def body(t, carry):
            j = 2 * t
            stage(j + 1, slot_b, False, j, slot_a)
            stage(j + 2, slot_a, False, j + 1, slot_b)
            return carry

        lax.fori_loop(0, n_loop, body, 0)
        j = 2 * n_loop

        @pl.when(i - j == 1)
        def _():
            stage(i, slot_b, True, j, slot_a)
            for hh in range(2):
                consume(i, slot_b, hh)

        @pl.when(i - j == 2)
        def _():
            stage(j + 1, slot_b, False, j, slot_a)
            stage(i, slot_a, True, j + 1, slot_b)
            for hh in range(2):
                consume(i, slot_a, hh)

    outs = []
    for hh in range(2):
        acc = acc_ref[hh]
        outs.append(acc[0:HEAD_DIM] * (1.0 / acc[HEAD_DIM:HEAD_DIM + 1]))
    return jnp.concatenate(outs, axis=0).T


def _attn_kernel(qt_ref, k_ref, vt_ref, kb_ref, o_ref, *scratch, tq):
    hp = pl.program_id(1)

    def query_block(i, carry):
        start = pl.multiple_of(i * tq, tq)
        out = _attn_query_block(i, hp, qt_ref[0, :, pl.ds(start, tq)], k_ref, vt_ref, kb_ref, *scratch)
        o_ref[0, pl.ds(start, tq), :] = out.astype(o_ref.dtype)
        return carry

    lax.fori_loop(0, qt_ref.shape[2] // tq, query_block, 0)


def _attention(qt, k, vt, kb, tq):
    B, S, _ = k.shape
    n_pairs = N_ATTN_HEADS // 2
    rows = pl.BlockSpec((1, S, LANES), lambda b, p: (b, 0, p))
    cols = pl.BlockSpec((1, LANES, S), lambda b, p: (b, p, 0))
    return pl.pallas_call(
        functools.partial(_attn_kernel, tq=tq),
        grid=(B, n_pairs),
        in_specs=[cols, rows, cols, pl.BlockSpec((1, S, LANES), lambda b, p: (b, 0, 0))],
        out_specs=rows,
        out_shape=jax.ShapeDtypeStruct((B, S, D_ATTN), BF16),
        scratch_shapes=[pltpu.VMEM((2, 2 * LANES, tq), BF16),
                        pltpu.VMEM((2, 1, tq), F32),
                        pltpu.VMEM((2, V_ROWS, tq), F32),
                        pltpu.VMEM((2, tq, tq), F32),
                        pltpu.VMEM((2, tq, tq), F32),
                        pltpu.VMEM((2, 1, tq), F32),
                        pltpu.VMEM((2, 1, tq), F32)],
        compiler_params=pltpu.CompilerParams(
            dimension_semantics=("arbitrary", "arbitrary"), vmem_limit_bytes=VMEM_LIMIT_BYTES),
        name="fox_attention",
    )(qt, k, vt, kb)


def _rglru_kernel(xr_ref, gr_ref, cw_ref, cb_ref, wbd_ref, ba_ref, bx_ref, lam_ref,
                  y_ref, xtail_ref, h_ref):
    ts = xr_ref.shape[1]
    width = xr_ref.shape[2]
    i = pl.program_id(1)

    @pl.when(i == 0)
    def _():
        xtail_ref[...] = jnp.zeros_like(xtail_ref)
        h_ref[...] = jnp.zeros_like(h_ref)

    n_groups = ts // 8
    x = xr_ref[0]
    x3 = x.reshape(n_groups, 8, width)
    tail = xtail_ref[...]
    sub = lax.broadcasted_iota(jnp.int32, (n_groups, 8, width), 1)
    xc3 = cb_ref[...] + cw_ref[RNN_CONV_W - 1:RNN_CONV_W, :] * x3
    for shift in range(1, RNN_CONV_W):
        kk = RNN_CONV_W - 1 - shift
        rot = pltpu.roll(x3, shift, axis=1)
        prev = jnp.concatenate([pltpu.roll(tail, shift, axis=0)[None], rot[:-1]], axis=0)
        xc3 = xc3 + cw_ref[kk:kk + 1, :] * jnp.where(sub < shift, prev, rot)
    xtail_ref[...] = x[ts - 8:ts]
    xc = xc3.reshape(ts, width)

    xcb = xc.astype(BF16)
    r_parts, i_parts = [], []
    for g in range(width // MXU_DIM):
        z = _dot(xcb[:, g * MXU_DIM:(g + 1) * MXU_DIM], wbd_ref[g])
        r_parts.append(z[:, 0:MXU_DIM])
        i_parts.append(z[:, MXU_DIM:2 * MXU_DIM])
    r = jax.nn.sigmoid(jnp.concatenate(r_parts, axis=1) + ba_ref[...])
    ig = jax.nn.sigmoid(jnp.concatenate(i_parts, axis=1) + bx_ref[...])
    log_a = (-RG_C) * r * _softplus(-lam_ref[...])
    a = jnp.exp(log_a)
    m2 = jnp.tanh(-log_a) * (1.0 + a * a)
    mult = jnp.where(m2 > 0.0, m2 * lax.rsqrt(m2), 0.0)
    row = lax.broadcasted_iota(jnp.int32, (ts, width), 0)
    mult = jnp.where(row == jnp.where(i == 0, 0, -1), 1.0, mult)
    b = mult * ig * xc

    a3 = a.reshape(n_groups, 8, width)
    b3 = b.reshape(n_groups, 8, width)
    for d in (1, 2, 4):
        valid = sub >= d
        b3 = jnp.where(valid, a3 * pltpu.roll(b3, d, axis=1), 0.0) + b3
        a3 = jnp.where(valid, a3 * pltpu.roll(a3, d, axis=1), a3)
    h_prev = h_ref[...]
    groups = []
    for g in range(n_groups):
        h_g = a3[g] * h_prev + b3[g]
        groups.append(h_g)
        h_prev = h_g[7:8]
    h_ref[...] = h_prev
    h = jnp.concatenate(groups, axis=0)
    y_ref[0] = (h * _gelu_tanh(gr_ref[0])).astype(y_ref.dtype)


def _rglru(xr, gr, cw, cb, wbd, ba, bx, lam, ts):
    B, S, W = xr.shape
    const = lambda shape: pl.BlockSpec(shape, lambda b, i: (0,) * len(shape))
    row = pl.BlockSpec((1, ts, W), lambda b, i: (b, i, 0))
    return pl.pallas_call(
        _rglru_kernel,
        grid=(B, S // ts),
        in_specs=[row, row, const(cw.shape), const((1, W)), const(wbd.shape),
                  const((1, W)), const((1, W)), const((1, W))],
        out_specs=row,
        out_shape=jax.ShapeDtypeStruct((B, S, W), BF16),
        scratch_shapes=[pltpu.VMEM((8, W), F32), pltpu.VMEM((1, W), F32)],
        compiler_params=pltpu.CompilerParams(
            dimension_semantics=("arbitrary", "arbitrary"), vmem_limit_bytes=VMEM_LIMIT_BYTES),
        name="rg_lru",
    )(xr, gr, cw, cb, wbd, ba, bx, lam)


def _outproj_kernel(ya_ref, yr_ref, x_ref, na_ref, nr_ref, wo_ref, npost_ref, nffn_ref,
                    x1_ref, h2_ref):
    tm, da = ya_ref.shape
    half = tm // 2
    for r in range(2):
        rows = pl.ds(r * half, half)
        ya = _rms(ya_ref[rows, :].astype(F32), na_ref[...]).astype(BF16)
        yr = _rms(yr_ref[rows, :].astype(F32), nr_ref[...]).astype(BF16)
        y = _dot(ya, wo_ref[0:da, :]) + _dot(yr, wo_ref[da:, :])
        x1 = x_ref[rows, :] + _rms(y, npost_ref[...])
        x1_ref[rows, :] = x1
        h2_ref[rows, :] = _rms(x1, nffn_ref[...]).astype(BF16)


def _out_projection(ya, yr, x, na, nr, wo, npost, nffn, tm):
    M, D = x.shape
    da, dr = ya.shape[1], yr.shape[1]
    const = lambda shape: pl.BlockSpec(shape, lambda i: (0,) * len(shape),
                                       pipeline_mode=pl.Buffered(1))
    row = lambda width: pl.BlockSpec((tm, width), lambda i: (i, 0))
    return pl.pallas_call(
        _outproj_kernel,
        grid=(M // tm,),
        in_specs=[row(da), row(dr), row(D), const((1, da)), const((1, dr)), const(wo.shape),
                  const((1, D)), const((1, D))],
        out_specs=[row(D), row(D)],
        out_shape=[jax.ShapeDtypeStruct((M, D), F32), jax.ShapeDtypeStruct((M, D), BF16)],
        compiler_params=pltpu.CompilerParams(
            dimension_semantics=("arbitrary",), vmem_limit_bytes=VMEM_LIMIT_BYTES),
        name="out_projection",
    )(ya, yr, x, na, nr, wo, npost, nffn)


def _ffn_kernel(h_ref, wg_ref, wu_ref, cw_ref, cb_ref, wd_ref, x1_ref, gain_ref,
                o_ref, acc_ref, gtail_ref, *, tiles_per_seq):
    tm = h_ref.shape[0]
    tc = wg_ref.shape[1]
    i = pl.program_id(0)
    c = pl.program_id(1)

    @pl.when(c == 0)
    def _():
        acc_ref[...] = jnp.zeros_like(acc_ref)

    h = h_ref[...]
    g = _dot(h, wg_ref[...])
    u = _dot(h, wu_ref[...])
    n_groups = tm // 8
    seq_start = (i % tiles_per_seq) == 0
    tail = jnp.where(seq_start, 0.0, gtail_ref[c])
    gtail_ref[c] = g[tm - 8:tm, :]
    g3 = g.reshape(n_groups, 8, tc)
    sub = lax.broadcasted_iota(jnp.int32, (n_groups, 8, tc), 1)
    conv = cb_ref[...] + cw_ref[FFN_CONV_W - 1:FFN_CONV_W, :] * g3
    for shift in range(1, FFN_CONV_W):
        kk = FFN_CONV_W - 1 - shift
        rot = pltpu.roll(g3, shift, axis=1)
        prev = jnp.concatenate([pltpu.roll(tail, shift, axis=0)[None], rot[:-1]], axis=0)
        conv = conv + cw_ref[kk:kk + 1, :] * jnp.where(sub < shift, prev, rot)
    act = (_gelu_tanh(conv.reshape(tm, tc)) * u).astype(BF16)
    acc_ref[...] += _dot(act, wd_ref[...])

    @pl.when(c == pl.num_programs(1) - 1)
    def _():
        o_ref[...] = x1_ref[...] + _rms(acc_ref[...], gain_ref[...])


def _ffn(h2, wg, wu, cw, cb, wd, x1, gain, tm, tc, tiles_per_seq):
    M, D = h2.shape
    dff = wg.shape[1]
    nc = dff // tc
    return pl.pallas_call(
        functools.partial(_ffn_kernel, tiles_per_seq=tiles_per_seq),
        grid=(M // tm, nc),
        in_specs=[pl.BlockSpec((tm, D), lambda i, c: (i, 0)),
                  pl.BlockSpec((D, tc), lambda i, c: (0, c)),
                  pl.BlockSpec((D, tc), lambda i, c: (0, c)),
                  pl.BlockSpec((FFN_CONV_W, tc), lambda i, c: (0, c)),
                  pl.BlockSpec((1, tc), lambda i, c: (0, c)),
                  pl.BlockSpec((tc, D), lambda i, c: (c, 0)),
                  pl.BlockSpec((tm, D), lambda i, c: (i, 0)),
                  pl.BlockSpec((1, D), lambda i, c: (0, 0))],
        out_specs=pl.BlockSpec((tm, D), lambda i, c: (i, 0)),
        out_shape=jax.ShapeDtypeStruct((M, D), F32),
        scratch_shapes=[pltpu.VMEM((tm, D), F32),
                        pltpu.VMEM((nc, 8, tc), F32)],
        compiler_params=pltpu.CompilerParams(
            dimension_semantics=("arbitrary", "arbitrary"), vmem_limit_bytes=VMEM_LIMIT_BYTES),
        name="conv_gated_mlp",
    )(h2, wg, wu, cw, cb, wd, x1, gain)


def _block_diag(w, group):
    n, r, _ = w.shape
    w = w.reshape(n // group, group, r, r)
    eye = jnp.eye(group, dtype=w.dtype)
    return jnp.einsum("gaij,ab->gaibj", w, eye).reshape(n // group, group * r, group * r)


def _layer(x, norm_mix_pre, w_in, b_forget, conv_rnn_w, conv_rnn_b, w_rg_a, b_rg_a, w_rg_x,
           b_rg_x, rg_lambda, norm_attn_out, norm_rnn_out, w_out, norm_mix_post, norm_ffn_pre,
           w_gate, w_up, conv_ffn_w, conv_ffn_b, w_down, norm_ffn_post):
    B, S, D = x.shape
    d_rnn = conv_rnn_w.shape[1]
    t_seq = min(512, S)

    nh = N_ATTN_HEADS
    w_q, w_k, w_v = w_in[:, 0:D_ATTN], w_in[:, D_ATTN:2 * D_ATTN], w_in[:, 2 * D_ATTN:3 * D_ATTN]
    w_f = w_in[:, 3 * D_ATTN:3 * D_ATTN + nh]
    w_xg = w_in[:, 3 * D_ATTN + nh:]
    w_f_rep = jnp.concatenate([w_f, w_f, w_f, jnp.zeros((D, LANES - 3 * nh), w_in.dtype)], axis=1)
    bf_rep = jnp.concatenate([b_forget, b_forget, b_forget,
                              jnp.zeros((LANES - 3 * nh,), b_forget.dtype)]).reshape(1, LANES)
    group = MXU_DIM // RNN_BLOCK
    wbd = jnp.concatenate([_block_diag(w_rg_a, group), _block_diag(w_rg_x, group)],
                          axis=2).astype(BF16)
    row = lambda v: v.reshape(1, -1)

    qt, k, vt, kb, xr, gr = _in_projection(
        x, row(norm_mix_pre), w_q.T.astype(BF16), w_v.T.astype(BF16), w_k.astype(BF16),
        w_xg.astype(BF16), w_f_rep.astype(BF16), bf_rep, t_seq)
    y_attn = _attention(qt, k, vt, kb, t_seq)
    y_rnn = _rglru(xr, gr, conv_rnn_w, row(conv_rnn_b), wbd, row(b_rg_a), row(b_rg_x),
                   row(rg_lambda), t_seq)
    M = B * S
    x1, h2 = _out_projection(y_attn.reshape(M, D_ATTN), y_rnn.reshape(M, d_rnn), x.reshape(M, D),
                             row(norm_attn_out), row(norm_rnn_out), w_out.astype(BF16),
                             row(norm_mix_post), row(norm_ffn_pre), t_seq)
    out = _ffn(h2, w_gate.astype(BF16), w_up.astype(BF16), conv_ffn_w, row(conv_ffn_b),
               w_down.astype(BF16), x1, row(norm_ffn_post), t_seq, 512, S // t_seq)
    return out.reshape(B, S, D)


def kernel(x, norm_mix_pre, w_in, b_forget, conv_rnn_w, conv_rnn_b, w_rg_a, b_rg_a, w_rg_x, b_rg_x, rg_lambda, norm_attn_out, norm_rnn_out, w_out, norm_mix_post, norm_ffn_pre, w_gate, w_up, conv_ffn_w, conv_ffn_b, w_down, norm_ffn_post):
    params = (norm_mix_pre, w_in, b_forget, conv_rnn_w, conv_rnn_b, w_rg_a, b_rg_a, w_rg_x, b_rg_x,
              rg_lambda, norm_attn_out, norm_rnn_out, w_out, norm_mix_post, norm_ffn_pre, w_gate,
              w_up, conv_ffn_w, conv_ffn_b, w_down, norm_ffn_post)
    for l in range(norm_mix_pre.shape[0]):
        x = _layer(x, *(p[l] for p in params))
    return x
```

```python
import functools
import math

import jax
import jax.numpy as jnp
from jax import lax
from jax.experimental import pallas as pl
from jax.experimental.pallas import tpu as pltpu

N_ATTN_HEADS = 16
HEAD_DIM = 64
D_ATTN = N_ATTN_HEADS * HEAD_DIM
N_RNN_BLOCKS = 16
RNN_BLOCK = 64
RNN_CONV_W = 4
RG_C = 8.0
FFN_CONV_W = 3
EPS = 1e-6
LOG2E = 1.4426950408889634

ZERO_EXP2 = -150.0
BOUND_SLACK = 8.0
NORM_SLACK = 1.05

LANES = 128
MXU_DIM = 256
VMEM_LIMIT_BYTES = 56 * 1024 * 1024

F32 = jnp.float32
BF16 = jnp.bfloat16


def _rms(x, gain):
    return x * lax.rsqrt(jnp.mean(x * x, axis=-1, keepdims=True) + EPS) * gain


def _gelu_tanh(x):
    cdf = 0.5 * (1.0 + jnp.tanh(math.sqrt(2.0 / math.pi) * (x + 0.044715 * (x * x * x))))
    return x * cdf


def _softplus(x):
    return jnp.maximum(x, 0.0) + jnp.log1p(jnp.exp(-jnp.abs(x)))


def _split3(x):
    hi = x.astype(BF16).astype(F32)
    r = x - hi
    mid = r.astype(BF16).astype(F32)
    lo = (r - mid).astype(BF16).astype(F32)
    return hi, mid, lo


def _dot(a, b):
    return jnp.dot(a, b, preferred_element_type=F32)


def _dot_nt(a, b):
    return lax.dot_general(a, b, (((1,), (1,)), ((), ())), preferred_element_type=F32)


def _proj_kernel(x_ref, g_ref, wqt_ref, wvt_ref, wk_ref, wxg_ref, wf_ref, bf_ref, hsel_ref,
                 qt_ref, k_ref, vt_ref, kb_ref, xr_ref, gr_ref, kst_ref, carry_ref):
    tm = x_ref.shape[1]

    @pl.when(pl.program_id(1) == 0)
    def _():
        carry_ref[...] = jnp.zeros_like(carry_ref)

    h = _rms(x_ref[0], g_ref[...]).astype(BF16)
    lane = lax.broadcasted_iota(jnp.int32, (tm, LANES), 1)

    fl = _dot(h, wf_ref[...]) + bf_ref[...]
    logf = jnp.minimum(fl, 0.0) - jnp.log1p(jnp.exp(-jnp.abs(fl)))
    logf = jnp.where(lane < 3 * N_ATTN_HEADS, logf, 0.0)
    hi, mid, lo = _split3(logf)
    pieces = jnp.concatenate([hi, mid, lo], axis=1).astype(BF16)

    qt_ref[0] = (_dot_nt(wqt_ref[...], h) * (HEAD_DIM ** -0.5 * LOG2E)).astype(BF16)

    rows = lax.broadcasted_iota(jnp.int32, (tm, tm), 0)
    cols = lax.broadcasted_iota(jnp.int32, (tm, tm), 1)
    tri = jnp.where(rows >= cols, 1.0, 0.0).astype(BF16)
    cs = _dot(tri, pieces)
    c = cs[:, 0:LANES] + cs[:, LANES:2 * LANES] + cs[:, 2 * LANES:3 * LANES] + carry_ref[...]
    carry_ref[...] = c[tm - 1:tm, :]
    bias = c * (-LOG2E)
    bhi, bmid, blo = _split3(bias)
    kb = jnp.where(lane < N_ATTN_HEADS, bhi, jnp.where(lane < 2 * N_ATTN_HEADS, bmid, blo))
    kb_ref[0] = kb.astype(BF16)

    vt_ref[0] = _dot_nt(wvt_ref[...], h).astype(BF16)
    k = _dot(h, wk_ref[...]).astype(BF16)
    k_ref[0] = k
    k2 = k.astype(F32)
    kn2 = _dot((k2 * k2).astype(BF16), hsel_ref[...])
    kst_ref[0, 0] = jnp.concatenate(
        [jnp.max(kn2, axis=0, keepdims=True), jnp.max(bias, axis=0, keepdims=True),
         jnp.zeros((6, LANES), F32)], axis=0)
    xr_ref[0] = _dot(h, wxg_ref[:, 0:D_ATTN])
    gr_ref[0] = _dot(h, wxg_ref[:, D_ATTN:2 * D_ATTN])


def _in_projection(x, gain, wqt, wvt, wk, wxg, wf, bf_rep, hsel, tm):
    B, S, D = x.shape
    const = lambda a: pl.BlockSpec(a.shape, lambda b, i: (0,) * a.ndim, pipeline_mode=pl.Buffered(1))
    row = lambda width: pl.BlockSpec((1, tm, width), lambda b, i: (b, i, 0))
    col = pl.BlockSpec((1, D_ATTN, tm), lambda b, i: (b, 0, i))
    weights = (gain, wqt, wvt, wk, wxg, wf, bf_rep, hsel)
    return pl.pallas_call(
        _proj_kernel,
        grid=(B, S // tm),
        in_specs=[row(D)] + [const(w) for w in weights],
        out_specs=[col, row(D_ATTN), col, row(LANES), row(D_ATTN), row(D_ATTN),
                   pl.BlockSpec((1, 1, 8, LANES), lambda b, i: (b, i, 0, 0))],
        out_shape=[jax.ShapeDtypeStruct((B, D_ATTN, S), BF16),
                   jax.ShapeDtypeStruct((B, S, D_ATTN), BF16),
                   jax.ShapeDtypeStruct((B, D_ATTN, S), BF16),
                   jax.ShapeDtypeStruct((B, S, LANES), BF16),
                   jax.ShapeDtypeStruct((B, S, D_ATTN), F32),
                   jax.ShapeDtypeStruct((B, S, D_ATTN), F32),
                   jax.ShapeDtypeStruct((B, S // tm, 8, LANES), F32)],
        scratch_shapes=[pltpu.VMEM((1, LANES), F32)],
        compiler_params=pltpu.CompilerParams(
            dimension_semantics=("arbitrary", "arbitrary"), vmem_limit_bytes=VMEM_LIMIT_BYTES),
        name="in_projection",
    )(x, *weights)


V_ROWS = HEAD_DIM + 16


def _attn_query_block(i, hp, qt, k_ref, vt_ref, kb_ref, kst_ref,
                      rhs_ref, m_ref, acc_ref, s_a_ref, s_b_ref, mb_a_ref, mb_b_ref, seq_ref):
    tq = qt.shape[1]
    tk = tq
    n_kblocks = k_ref.shape[1] // tk

    row = lax.broadcasted_iota(jnp.int32, (LANES, tq), 0)
    for hh in range(2):
        head = 2 * hp + hh
        qh = jnp.where((row >= hh * HEAD_DIM) & (row < (hh + 1) * HEAD_DIM), qt, jnp.zeros_like(qt))
        sel = (row == head) | (row == head + N_ATTN_HEADS) | (row == head + 2 * N_ATTN_HEADS)
        rhs_ref[hh, 0:LANES, :] = qh
        rhs_ref[hh, LANES:2 * LANES, :] = jnp.where(sel, 1.0, 0.0).astype(BF16)
    m_ref[...] = jnp.full_like(m_ref, -jnp.inf)
    acc_ref[...] = jnp.zeros_like(acc_ref)
    ones = jnp.ones((V_ROWS - HEAD_DIM, tk), BF16)

    def scores(j, slot, diagonal, hh):
        s_ref, mb_ref = slot
        start = pl.multiple_of(j * tk, tk)
        lhs = jnp.concatenate([k_ref[0, pl.ds(start, tk), :], kb_ref[0, pl.ds(start, tk), :]], axis=1)
        s = _dot(lhs, rhs_ref[hh])
        if diagonal:
            key = lax.broadcasted_iota(jnp.int32, (tk, tq), 0)
            qry = lax.broadcasted_iota(jnp.int32, (tk, tq), 1)
            s = jnp.where(key <= qry, s, -jnp.inf)
        s_ref[hh] = s
        mb_ref[hh] = jnp.max(s, axis=0, keepdims=True)

    def consume(j, slot, hh):
        s_ref, mb_ref = slot
        start = pl.multiple_of(j * tk, tk)
        m_old = m_ref[hh]
        m_new = jnp.maximum(m_old, mb_ref[hh])
        p = jnp.exp2(s_ref[hh] - m_new).astype(BF16)
        vt = jnp.concatenate(
            [vt_ref[0, hh * HEAD_DIM:(hh + 1) * HEAD_DIM, pl.ds(start, tk)], ones], axis=0)
        acc_ref[hh] = acc_ref[hh] * jnp.exp2(m_old - m_new) + _dot(vt, p)
        m_ref[hh] = m_new

    def stage(js, slot_s, jc, slot_c):
        for hh in range(2):
            scores(js, slot_s, False, hh)
            consume(jc, slot_c, hh)

    slot_a = (s_a_ref, mb_a_ref)
    slot_b = (s_b_ref, mb_b_ref)

    for hh in range(2):
        scores(i, slot_a, True, hh)

    lane = lax.broadcasted_iota(jnp.int32, (1, LANES), 1)
    need = jnp.zeros((1, LANES), jnp.int32)
    for hh in range(2):
        qf = qt[hh * HEAD_DIM:(hh + 1) * HEAD_DIM, :].astype(F32)
        qn2 = jnp.max(jnp.sum(qf * qf, axis=0, keepdims=True), axis=1, keepdims=True)
        m_min = jnp.min(mb_a_ref[hh], axis=1, keepdims=True)
        kn2 = kst_ref[0, 0, hh:hh + 1, :]
        bias_max = kst_ref[0, 0, 2 + hh:3 + hh, :]
        bound = jnp.sqrt(qn2 * kn2 * NORM_SLACK) + bias_max - m_min
        need = need | jnp.where(bound >= ZERO_EXP2 - BOUND_SLACK, 1, 0)
    need = jnp.where(lane < i, need, 0)
    seq_ref[0] = i
    count = jnp.int32(1)
    for j in range(n_kblocks - 2, -1, -1):
        seq_ref[count] = j
        count = count + need[0, j]
    last = count - 1

    def body(t, carry):
        n = 2 * t
        stage(seq_ref[n + 1], slot_b, seq_ref[n], slot_a)
        stage(seq_ref[n + 2], slot_a, seq_ref[n + 1], slot_b)
        return carry

    lax.fori_loop(0, last // 2, body, 0)

    @pl.when(last % 2 == 0)
    def _():
        for hh in range(2):
            consume(seq_ref[last], slot_a, hh)

    @pl.when(last % 2 == 1)
    def _():
        stage(seq_ref[last], slot_b, seq_ref[last - 1], slot_a)
        for hh in range(2):
            consume(seq_ref[last], slot_b, hh)

    outs = []
    for hh in range(2):
        acc = acc_ref[hh]
        outs.append(acc[0:HEAD_DIM] * (1.0 / acc[HEAD_DIM:HEAD_DIM + 1]))
    return jnp.concatenate(outs, axis=0).T


def _attn_kernel(qt_ref, k_ref, vt_ref, kb_ref, kst_ref, o_ref, *scratch, tq):
    hp = pl.program_id(1)

    def query_block(i, carry):
        start = pl.multiple_of(i * tq, tq)
        out = _attn_query_block(i, hp, qt_ref[0, :, pl.ds(start, tq)], k_ref, vt_ref, kb_ref, kst_ref,
                                *scratch)
        o_ref[0, pl.ds(start, tq), :] = out.astype(o_ref.dtype)
        return carry

    lax.fori_loop(0, qt_ref.shape[2] // tq, query_block, 0)


def _attention(qt, k, vt, kb, kst, tq):
    B, S, _ = k.shape
    n_pairs = N_ATTN_HEADS // 2
    rows = pl.BlockSpec((1, S, LANES), lambda b, p: (b, 0, p))
    cols = pl.BlockSpec((1, LANES, S), lambda b, p: (b, p, 0))
    return pl.pallas_call(
        functools.partial(_attn_kernel, tq=tq),
        grid=(B, n_pairs),
        in_specs=[cols, rows, cols, pl.BlockSpec((1, S, LANES), lambda b, p: (b, 0, 0)),
                  pl.BlockSpec((1, 1, 8, LANES), lambda b, p: (b, p, 0, 0))],
        out_specs=rows,
        out_shape=jax.ShapeDtypeStruct((B, S, D_ATTN), BF16),
        scratch_shapes=[pltpu.VMEM((2, 2 * LANES, tq), BF16),
                        pltpu.VMEM((2, 1, tq), F32),
                        pltpu.VMEM((2, V_ROWS, tq), F32),
                        pltpu.VMEM((2, tq, tq), F32),
                        pltpu.VMEM((2, tq, tq), F32),
                        pltpu.VMEM((2, 1, tq), F32),
                        pltpu.VMEM((2, 1, tq), F32),
                        pltpu.SMEM((S // tq + 1,), jnp.int32)],
        compiler_params=pltpu.CompilerParams(
            dimension_semantics=("arbitrary", "arbitrary"), vmem_limit_bytes=VMEM_LIMIT_BYTES),
        name="fox_attention",
    )(qt, k, vt, kb, kst)


def _rglru_kernel(xr_ref, gr_ref, cw_ref, cb_ref, wbd_ref, ba_ref, bx_ref, lam_ref,
                  y_ref, xtail_ref, h_ref):
    ts = xr_ref.shape[1]
    width = xr_ref.shape[2]
    i = pl.program_id(1)

    @pl.when(i == 0)
    def _():
        xtail_ref[...] = jnp.zeros_like(xtail_ref)
        h_ref[...] = jnp.zeros_like(h_ref)

    n_groups = ts // 8
    x = xr_ref[0]
    x3 = x.reshape(n_groups, 8, width)
    tail = xtail_ref[...]
    sub = lax.broadcasted_iota(jnp.int32, (n_groups, 8, width), 1)
    xc3 = cb_ref[...] + cw_ref[RNN_CONV_W - 1:RNN_CONV_W, :] * x3
    for shift in range(1, RNN_CONV_W):
        kk = RNN_CONV_W - 1 - shift
        rot = pltpu.roll(x3, shift, axis=1)
        prev = jnp.concatenate([pltpu.roll(tail, shift, axis=0)[None], rot[:-1]], axis=0)
        xc3 = xc3 + cw_ref[kk:kk + 1, :] * jnp.where(sub < shift, prev, rot)
    xtail_ref[...] = x[ts - 8:ts]
    xc = xc3.reshape(ts, width)

    xcb = xc.astype(BF16)
    r_parts, i_parts = [], []
    for g in range(width // MXU_DIM):
        z = _dot(xcb[:, g * MXU_DIM:(g + 1) * MXU_DIM], wbd_ref[g])
        r_parts.append(z[:, 0:MXU_DIM])
        i_parts.append(z[:, MXU_DIM:2 * MXU_DIM])
    r = jax.nn.sigmoid(jnp.concatenate(r_parts, axis=1) + ba_ref[...])
    ig = jax.nn.sigmoid(jnp.concatenate(i_parts, axis=1) + bx_ref[...])
    log_a = (-RG_C) * r * _softplus(-lam_ref[...])
    a = jnp.exp(log_a)
    m2 = jnp.tanh(-log_a) * (1.0 + a * a)
    mult = jnp.where(m2 > 0.0, m2 * lax.rsqrt(m2), 0.0)
    row = lax.broadcasted_iota(jnp.int32, (ts, width), 0)
    mult = jnp.where(row == jnp.where(i == 0, 0, -1), 1.0, mult)
    b = mult * ig * xc

    a3 = a.reshape(n_groups, 8, width)
    b3 = b.reshape(n_groups, 8, width)
    for d in (1, 2, 4):
        valid = sub >= d
        b3 = jnp.where(valid, a3 * pltpu.roll(b3, d, axis=1), 0.0) + b3
        a3 = jnp.where(valid, a3 * pltpu.roll(a3, d, axis=1), a3)
    h_prev = h_ref[...]
    groups = []
    for g in range(n_groups):
        h_g = a3[g] * h_prev + b3[g]
        groups.append(h_g)
        h_prev = h_g[7:8]
    h_ref[...] = h_prev
    h = jnp.concatenate(groups, axis=0)
    y_ref[0] = (h * _gelu_tanh(gr_ref[0])).astype(y_ref.dtype)


def _rglru(xr, gr, cw, cb, wbd, ba, bx, lam, ts):
    B, S, W = xr.shape
    const = lambda shape: pl.BlockSpec(shape, lambda b, i: (0,) * len(shape))
    row = pl.BlockSpec((1, ts, W), lambda b, i: (b, i, 0))
    return pl.pallas_call(
        _rglru_kernel,
        grid=(B, S // ts),
        in_specs=[row, row, const(cw.shape), const((1, W)), const(wbd.shape),
                  const((1, W)), const((1, W)), const((1, W))],
        out_specs=row,
        out_shape=jax.ShapeDtypeStruct((B, S, W), BF16),
        scratch_shapes=[pltpu.VMEM((8, W), F32), pltpu.VMEM((1, W), F32)],
        compiler_params=pltpu.CompilerParams(
            dimension_semantics=("arbitrary", "arbitrary"), vmem_limit_bytes=VMEM_LIMIT_BYTES),
        name="rg_lru",
    )(xr, gr, cw, cb, wbd, ba, bx, lam)


def _outproj_kernel(ya_ref, yr_ref, x_ref, na_ref, nr_ref, wo_ref, npost_ref, nffn_ref,
                    x1_ref, h2_ref):
    tm, da = ya_ref.shape
    half = tm // 2
    for r in range(2):
        rows = pl.ds(r * half, half)
        ya = _rms(ya_ref[rows, :].astype(F32), na_ref[...]).astype(BF16)
        yr = _rms(yr_ref[rows, :].astype(F32), nr_ref[...]).astype(BF16)
        y = _dot(ya, wo_ref[0:da, :]) + _dot(yr, wo_ref[da:, :])
        x1 = x_ref[rows, :] + _rms(y, npost_ref[...])
        x1_ref[rows, :] = x1
        h2_ref[rows, :] = _rms(x1, nffn_ref[...]).astype(BF16)


def _out_projection(ya, yr, x, na, nr, wo, npost, nffn, tm):
    M, D = x.shape
    da, dr = ya.shape[1], yr.shape[1]
    const = lambda shape: pl.BlockSpec(shape, lambda i: (0,) * len(shape),
                                       pipeline_mode=pl.Buffered(1))
    row = lambda width: pl.BlockSpec((tm, width), lambda i: (i, 0))
    return pl.pallas_call(
        _outproj_kernel,
        grid=(M // tm,),
        in_specs=[row(da), row(dr), row(D), const((1, da)), const((1, dr)), const(wo.shape),
                  const((1, D)), const((1, D))],
        out_specs=[row(D), row(D)],
        out_shape=[jax.ShapeDtypeStruct((M, D), F32), jax.ShapeDtypeStruct((M, D), BF16)],
        compiler_params=pltpu.CompilerParams(
            dimension_semantics=("arbitrary",), vmem_limit_bytes=VMEM_LIMIT_BYTES),
        name="out_projection",
    )(ya, yr, x, na, nr, wo, npost, nffn)


def _ffn_kernel(h_ref, wg_ref, wu_ref, cw_ref, cb_ref, wd_ref, x1_ref, gain_ref,
                o_ref, acc_ref, gtail_ref, *, tiles_per_seq):
    tm = h_ref.shape[0]
    tc = wg_ref.shape[1]
    i = pl.program_id(0)
    c = pl.program_id(1)

    @pl.when(c == 0)
    def _():
        acc_ref[...] = jnp.zeros_like(acc_ref)

    h = h_ref[...]
    g = _dot(h, wg_ref[...])
    u = _dot(h, wu_ref[...])
    n_groups = tm // 8
    seq_start = (i % tiles_per_seq) == 0
    tail = jnp.where(seq_start, 0.0, gtail_ref[c])
    gtail_ref[c] = g[tm - 8:tm, :]
    g3 = g.reshape(n_groups, 8, tc)
    sub = lax.broadcasted_iota(jnp.int32, (n_groups, 8, tc), 1)
    conv = cb_ref[...] + cw_ref[FFN_CONV_W - 1:FFN_CONV_W, :] * g3
    for shift in range(1, FFN_CONV_W):
        kk = FFN_CONV_W - 1 - shift
        rot = pltpu.roll(g3, shift, axis=1)
        prev = jnp.concatenate([pltpu.roll(tail, shift, axis=0)[None], rot[:-1]], axis=0)
        conv = conv + cw_ref[kk:kk + 1, :] * jnp.where(sub < shift, prev, rot)
    act = (_gelu_tanh(conv.reshape(tm, tc)) * u).astype(BF16)
    acc_ref[...] += _dot(act, wd_ref[...])

    @pl.when(c == pl.num_programs(1) - 1)
    def _():
        o_ref[...] = x1_ref[...] + _rms(acc_ref[...], gain_ref[...])


def _ffn(h2, wg, wu, cw, cb, wd, x1, gain, tm, tc, tiles_per_seq):
    M, D = h2.shape
    dff = wg.shape[1]
    nc = dff // tc
    return pl.pallas_call(
        functools.partial(_ffn_kernel, tiles_per_seq=tiles_per_seq),
        grid=(M // tm, nc),
        in_specs=[pl.BlockSpec((tm, D), lambda i, c: (i, 0)),
                  pl.BlockSpec((D, tc), lambda i, c: (0, c)),
                  pl.BlockSpec((D, tc), lambda i, c: (0, c)),
                  pl.BlockSpec((FFN_CONV_W, tc), lambda i, c: (0, c)),
                  pl.BlockSpec((1, tc), lambda i, c: (0, c)),
                  pl.BlockSpec((tc, D), lambda i, c: (c, 0)),
                  pl.BlockSpec((tm, D), lambda i, c: (i, 0)),
                  pl.BlockSpec((1, D), lambda i, c: (0, 0))],
        out_specs=pl.BlockSpec((tm, D), lambda i, c: (i, 0)),
        out_shape=jax.ShapeDtypeStruct((M, D), F32),
        scratch_shapes=[pltpu.VMEM((tm, D), F32),
                        pltpu.VMEM((nc, 8, tc), F32)],
        compiler_params=pltpu.CompilerParams(
            dimension_semantics=("arbitrary", "arbitrary"), vmem_limit_bytes=VMEM_LIMIT_BYTES),
        name="conv_gated_mlp",
    )(h2, wg, wu, cw, cb, wd, x1, gain)


def _block_diag(w, group):
    n, r, _ = w.shape
    w = w.reshape(n // group, group, r, r)
    eye = jnp.eye(group, dtype=w.dtype)
    return jnp.einsum("gaij,ab->gaibj", w, eye).reshape(n // group, group * r, group * r)


def _layer(x, norm_mix_pre, w_in, b_forget, conv_rnn_w, conv_rnn_b, w_rg_a, b_rg_a, w_rg_x,
           b_rg_x, rg_lambda, norm_attn_out, norm_rnn_out, w_out, norm_mix_post, norm_ffn_pre,
           w_gate, w_up, conv_ffn_w, conv_ffn_b, w_down, norm_ffn_post):
    B, S, D = x.shape
    d_rnn = conv_rnn_w.shape[1]
    t_seq = min(512, S)

    nh = N_ATTN_HEADS
    w_q, w_k, w_v = w_in[:, 0:D_ATTN], w_in[:, D_ATTN:2 * D_ATTN], w_in[:, 2 * D_ATTN:3 * D_ATTN]
    w_f = w_in[:, 3 * D_ATTN:3 * D_ATTN + nh]
    w_xg = w_in[:, 3 * D_ATTN + nh:]
    w_f_rep = jnp.concatenate([w_f, w_f, w_f, jnp.zeros((D, LANES - 3 * nh), w_in.dtype)], axis=1)
    bf_rep = jnp.concatenate([b_forget, b_forget, b_forget,
                              jnp.zeros((LANES - 3 * nh,), b_forget.dtype)]).reshape(1, LANES)
    group = MXU_DIM // RNN_BLOCK
    wbd = jnp.concatenate([_block_diag(w_rg_a, group), _block_diag(w_rg_x, group)],
                          axis=2).astype(BF16)
    row = lambda v: v.reshape(1, -1)

    hsel = jnp.repeat(jnp.eye(nh, LANES, dtype=BF16), HEAD_DIM, axis=0)
    qt, k, vt, kb, xr, gr, kst = _in_projection(
        x, row(norm_mix_pre), w_q.T.astype(BF16), w_v.T.astype(BF16), w_k.astype(BF16),
        w_xg.astype(BF16), w_f_rep.astype(BF16), bf_rep, hsel, t_seq)
    n_blocks = S // t_seq
    kst = kst[:, :, 0:2, 0:nh].reshape(B, n_blocks, 2, nh // 2, 2).transpose(0, 3, 2, 4, 1)
    kst = jnp.pad(kst.reshape(B, nh // 2, 4, n_blocks), ((0, 0), (0, 0), (0, 4), (0, LANES - n_blocks)))
    y_attn = _attention(qt, k, vt, kb, kst, t_seq)
    y_rnn = _rglru(xr, gr, conv_rnn_w, row(conv_rnn_b), wbd, row(b_rg_a), row(b_rg_x),
                   row(rg_lambda), t_seq)
    M = B * S
    x1, h2 = _out_projection(y_attn.reshape(M, D_ATTN), y_rnn.reshape(M, d_rnn), x.reshape(M, D),
                             row(norm_attn_out), row(norm_rnn_out), w_out.astype(BF16),
                             row(norm_mix_post), row(norm_ffn_pre), t_seq)
    out = _ffn(h2, w_gate.astype(BF16), w_up.astype(BF16), conv_ffn_w, row(conv_ffn_b),
               w_down.astype(BF16), x1, row(norm_ffn_post), t_seq, 512, S // t_seq)
    return out.reshape(B, S, D)


def kernel(x, norm_mix_pre, w_in, b_forget, conv_rnn_w, conv_rnn_b, w_rg_a, b_rg_a, w_rg_x, b_rg_x, rg_lambda, norm_attn_out, norm_rnn_out, w_out, norm_mix_post, norm_ffn_pre, w_gate, w_up, conv_ffn_w, conv_ffn_b, w_down, norm_ffn_post):
    params = (norm_mix_pre, w_in, b_forget, conv_rnn_w, conv_rnn_b, w_rg_a, b_rg_a, w_rg_x, b_rg_x,
              rg_lambda, norm_attn_out, norm_rnn_out, w_out, norm_mix_post, norm_ffn_pre, w_gate,
              w_up, conv_ffn_w, conv_ffn_b, w_down, norm_ffn_post)
    for l in range(norm_mix_pre.shape[0]):
        x = _layer(x, *(p[l] for p in params))
    return x
```

```python
import functools
import math

import jax
import jax.numpy as jnp
from jax import lax
from jax.experimental import pallas as pl
from jax.experimental.pallas import tpu as pltpu

N_ATTN_HEADS = 16
HEAD_DIM = 64
D_ATTN = N_ATTN_HEADS * HEAD_DIM
N_RNN_BLOCKS = 16
RNN_BLOCK = 64
RNN_CONV_W = 4
RG_C = 8.0
FFN_CONV_W = 3
EPS = 1e-6
LOG2E = 1.4426950408889634

ZERO_EXP2 = -150.0
BOUND_SLACK = 8.0
NORM_SLACK = 1.05

LANES = 128
MXU_DIM = 256
VMEM_LIMIT_BYTES = 58 * 1024 * 1024

F32 = jnp.float32
BF16 = jnp.bfloat16


def _rms(x, gain):
    return x * lax.rsqrt(jnp.mean(x * x, axis=-1, keepdims=True) + EPS) * gain


def _gelu_tanh(x):
    cdf = 0.5 * (1.0 + jnp.tanh(math.sqrt(2.0 / math.pi) * (x + 0.044715 * (x * x * x))))
    return x * cdf


def _softplus(x):
    return jnp.maximum(x, 0.0) + jnp.log1p(jnp.exp(-jnp.abs(x)))


def _split3(x):
    hi = x.astype(BF16).astype(F32)
    r = x - hi
    mid = r.astype(BF16).astype(F32)
    lo = (r - mid).astype(BF16).astype(F32)
    return hi, mid, lo


def _dot(a, b):
    return jnp.dot(a, b, preferred_element_type=F32)


def _dot_nt(a, b):
    return lax.dot_general(a, b, (((1,), (1,)), ((), ())), preferred_element_type=F32)


def _proj_kernel(x_ref, g_ref, wqt_ref, wvt_ref, wk_ref, wxg_ref, wf_ref, bf_ref, hsel_ref,
                 qt_ref, k_ref, vt_ref, kb_ref, xr_ref, gr_ref, kst_ref, carry_ref):
    tm = x_ref.shape[1]

    @pl.when(pl.program_id(1) == 0)
    def _():
        carry_ref[...] = jnp.zeros_like(carry_ref)

    h = _rms(x_ref[0], g_ref[...]).astype(BF16)
    lane = lax.broadcasted_iota(jnp.int32, (tm, LANES), 1)

    fl = _dot(h, wf_ref[...]) + bf_ref[...]
    logf = jnp.minimum(fl, 0.0) - jnp.log1p(jnp.exp(-jnp.abs(fl)))
    logf = jnp.where(lane < 3 * N_ATTN_HEADS, logf, 0.0)
    hi, mid, lo = _split3(logf)
    pieces = jnp.concatenate([hi, mid, lo], axis=1).astype(BF16)

    qt_ref[0] = (_dot_nt(wqt_ref[...], h) * (HEAD_DIM ** -0.5 * LOG2E)).astype(BF16)

    rows = lax.broadcasted_iota(jnp.int32, (tm, tm), 0)
    cols = lax.broadcasted_iota(jnp.int32, (tm, tm), 1)
    tri = jnp.where(rows >= cols, 1.0, 0.0).astype(BF16)
    cs = _dot(tri, pieces)
    c = cs[:, 0:LANES] + cs[:, LANES:2 * LANES] + cs[:, 2 * LANES:3 * LANES] + carry_ref[...]
    carry_ref[...] = c[tm - 1:tm, :]
    bias = c * (-LOG2E)
    bhi, bmid, blo = _split3(bias)
    kb = jnp.where(lane < N_ATTN_HEADS, bhi, jnp.where(lane < 2 * N_ATTN_HEADS, bmid, blo))
    kb_ref[0] = kb.astype(BF16)

    vt_ref[0] = _dot_nt(wvt_ref[...], h).astype(BF16)
    k = _dot(h, wk_ref[...]).astype(BF16)
    k_ref[0] = k
    k2 = k.astype(F32)
    kn2 = _dot((k2 * k2).astype(BF16), hsel_ref[...])
    kst_ref[0, 0] = jnp.concatenate(
        [jnp.max(kn2, axis=0, keepdims=True), jnp.max(bias, axis=0, keepdims=True),
         jnp.zeros((6, LANES), F32)], axis=0)
    xr_ref[0] = _dot(h, wxg_ref[:, 0:D_ATTN])
    gr_ref[0] = _dot(h, wxg_ref[:, D_ATTN:2 * D_ATTN])


def _in_projection(x, gain, wqt, wvt, wk, wxg, wf, bf_rep, hsel, tm):
    B, S, D = x.shape
    const = lambda a: pl.BlockSpec(a.shape, lambda b, i: (0,) * a.ndim, pipeline_mode=pl.Buffered(1))
    row = lambda width: pl.BlockSpec((1, tm, width), lambda b, i: (b, i, 0))
    col = pl.BlockSpec((1, D_ATTN, tm), lambda b, i: (b, 0, i))
    weights = (gain, wqt, wvt, wk, wxg, wf, bf_rep, hsel)
    return pl.pallas_call(
        _proj_kernel,
        grid=(B, S // tm),
        in_specs=[row(D)] + [const(w) for w in weights],
        out_specs=[col, row(D_ATTN), col, row(LANES), row(D_ATTN), row(D_ATTN),
                   pl.BlockSpec((1, 1, 8, LANES), lambda b, i: (b, i, 0, 0))],
        out_shape=[jax.ShapeDtypeStruct((B, D_ATTN, S), BF16),
                   jax.ShapeDtypeStruct((B, S, D_ATTN), BF16),
                   jax.ShapeDtypeStruct((B, D_ATTN, S), BF16),
                   jax.ShapeDtypeStruct((B, S, LANES), BF16),
                   jax.ShapeDtypeStruct((B, S, D_ATTN), F32),
                   jax.ShapeDtypeStruct((B, S, D_ATTN), F32),
                   jax.ShapeDtypeStruct((B, S // tm, 8, LANES), F32)],
        scratch_shapes=[pltpu.VMEM((1, LANES), F32)],
        compiler_params=pltpu.CompilerParams(
            dimension_semantics=("arbitrary", "arbitrary"), vmem_limit_bytes=VMEM_LIMIT_BYTES),
        name="in_projection",
    )(x, *weights)


V_ROWS = HEAD_DIM + 16


def _attn_kernel(qt_ref, k_ref, vt_ref, kb_ref, kst_ref, o_ref,
                 rhs_ref, m_ref, acc_ref, s_a_ref, s_b_ref, s_c_ref, mb_a_ref, mb_b_ref, mb_c_ref,
                 seq_ref, *, tq):
    hp = pl.program_id(1)
    tk = tq
    n_kblocks = k_ref.shape[1] // tk
    n_qblocks = qt_ref.shape[2] // tq
    ones = jnp.ones((V_ROWS - HEAD_DIM, tk), BF16)
    slot_a = (s_a_ref, mb_a_ref)
    slot_b = (s_b_ref, mb_b_ref)
    slot_c = (s_c_ref, mb_c_ref)

    def q_block(i):
        return qt_ref[0, :, pl.ds(pl.multiple_of(i * tq, tq), tq)]

    def scores(j, slot, diagonal, hh):
        s_ref, mb_ref = slot
        start = pl.multiple_of(j * tk, tk)
        lhs = jnp.concatenate([k_ref[0, pl.ds(start, tk), :], kb_ref[0, pl.ds(start, tk), :]], axis=1)
        s = _dot(lhs, rhs_ref[hh])
        if diagonal:
            key = lax.broadcasted_iota(jnp.int32, (tk, tq), 0)
            qry = lax.broadcasted_iota(jnp.int32, (tk, tq), 1)
            s = jnp.where(key <= qry, s, -jnp.inf)
        s_ref[hh] = s
        mb_ref[hh] = jnp.max(s, axis=0, keepdims=True)

    def consume(j, slot, hh):
        s_ref, mb_ref = slot
        start = pl.multiple_of(j * tk, tk)
        m_old = m_ref[hh]
        m_new = jnp.maximum(m_old, mb_ref[hh])
        p = jnp.exp2(s_ref[hh] - m_new).astype(BF16)
        vt = jnp.concatenate(
            [vt_ref[0, hh * HEAD_DIM:(hh + 1) * HEAD_DIM, pl.ds(start, tk)], ones], axis=0)
        acc_ref[hh] = acc_ref[hh] * jnp.exp2(m_old - m_new) + _dot(vt, p)
        m_ref[hh] = m_new

    def stage(js, slot_s, jc, slot_prev):
        for hh in range(2):
            scores(js, slot_s, False, hh)
            consume(jc, slot_prev, hh)

    def fill(i):
        qt = q_block(i)
        row = lax.broadcasted_iota(jnp.int32, (LANES, tq), 0)
        for hh in range(2):
            head = 2 * hp + hh
            qh = jnp.where((row >= hh * HEAD_DIM) & (row < (hh + 1) * HEAD_DIM), qt, jnp.zeros_like(qt))
            sel = (row == head) | (row == head + N_ATTN_HEADS) | (row == head + 2 * N_ATTN_HEADS)
            rhs_ref[hh, 0:LANES, :] = qh
            rhs_ref[hh, LANES:2 * LANES, :] = jnp.where(sel, 1.0, 0.0).astype(BF16)
        for hh in range(2):
            scores(i, slot_c, True, hh)

    def plan(i):
        qt = q_block(i)
        lane = lax.broadcasted_iota(jnp.int32, (1, LANES), 1)
        need = jnp.zeros((1, LANES), jnp.int32)
        for hh in range(2):
            qf = qt[hh * HEAD_DIM:(hh + 1) * HEAD_DIM, :].astype(F32)
            qn2 = jnp.max(jnp.sum(qf * qf, axis=0, keepdims=True), axis=1, keepdims=True)
            m_min = jnp.min(mb_c_ref[hh], axis=1, keepdims=True)
            kn2 = kst_ref[0, 0, hh:hh + 1, :]
            bias_max = kst_ref[0, 0, 2 + hh:3 + hh, :]
            bound = jnp.sqrt(qn2 * kn2 * NORM_SLACK) + bias_max - m_min
            need = need | jnp.where(bound >= ZERO_EXP2 - BOUND_SLACK, 1, 0)
        need = jnp.where(lane < i, need, 0)
        seq_ref[0] = i
        count = jnp.int32(1)
        for j in range(n_kblocks - 2, -1, -1):
            seq_ref[count] = j
            count = count + need[0, j]
        return count - 1

    def finish(i, last, slot):
        for hh in range(2):
            consume(seq_ref[last], slot, hh)
        outs = []
        for hh in range(2):
            acc = acc_ref[hh]
            outs.append(acc[0:HEAD_DIM] * (1.0 / acc[HEAD_DIM:HEAD_DIM + 1]))
        out = jnp.concatenate(outs, axis=0).T
        o_ref[0, pl.ds(pl.multiple_of(i * tq, tq), tq), :] = out.astype(o_ref.dtype)
        fill(jnp.minimum(i + 1, n_qblocks - 1))

    def query_block(i, carry):
        last = plan(i)
        m_ref[...] = jnp.full_like(m_ref, -jnp.inf)
        acc_ref[...] = jnp.zeros_like(acc_ref)

        @pl.when(last == 0)
        def _():
            finish(i, last, slot_c)

        @pl.when(last > 0)
        def _():
            stage(seq_ref[1], slot_b, seq_ref[0], slot_c)
            n_loop = (last - 1) // 2

            def body(t, carry):
                n = 1 + 2 * t
                stage(seq_ref[n + 1], slot_a, seq_ref[n], slot_b)
                stage(seq_ref[n + 2], slot_b, seq_ref[n + 1], slot_a)
                return carry

            lax.fori_loop(0, n_loop, body, 0)
            n = 1 + 2 * n_loop

            @pl.when(last == n)
            def _():
                finish(i, last, slot_b)

            @pl.when(last == n + 1)
            def _():
                stage(seq_ref[last], slot_a, seq_ref[n], slot_b)
                finish(i, last, slot_a)

        return carry

    fill(0)
    lax.fori_loop(0, n_qblocks, query_block, 0)


def _attention(qt, k, vt, kb, kst, tq):
    B, S, _ = k.shape
    n_pairs = N_ATTN_HEADS // 2
    rows = pl.BlockSpec((1, S, LANES), lambda b, p: (b, 0, p))
    cols = pl.BlockSpec((1, LANES, S), lambda b, p: (b, p, 0))
    return pl.pallas_call(
        functools.partial(_attn_kernel, tq=tq),
        grid=(B, n_pairs),
        in_specs=[cols, rows, cols, pl.BlockSpec((1, S, LANES), lambda b, p: (b, 0, 0)),
                  pl.BlockSpec((1, 1, 8, LANES), lambda b, p: (b, p, 0, 0))],
        out_specs=rows,
        out_shape=jax.ShapeDtypeStruct((B, S, D_ATTN), BF16),
        scratch_shapes=[pltpu.VMEM((2, 2 * LANES, tq), BF16),
                        pltpu.VMEM((2, 1, tq), F32),
                        pltpu.VMEM((2, V_ROWS, tq), F32),
                        pltpu.VMEM((2, tq, tq), F32),
                        pltpu.VMEM((2, tq, tq), F32),
                        pltpu.VMEM((2, tq, tq), F32),
                        pltpu.VMEM((2, 1, tq), F32),
                        pltpu.VMEM((2, 1, tq), F32),
                        pltpu.VMEM((2, 1, tq), F32),
                        pltpu.SMEM((S // tq + 1,), jnp.int32)],
        compiler_params=pltpu.CompilerParams(
            dimension_semantics=("arbitrary", "arbitrary"), vmem_limit_bytes=VMEM_LIMIT_BYTES),
        name="fox_attention",
    )(qt, k, vt, kb, kst)


def _rglru_kernel(xr_ref, gr_ref, cw_ref, cb_ref, wbd_ref, ba_ref, bx_ref, lam_ref,
                  y_ref, xtail_ref, h_ref):
    ts = xr_ref.shape[1]
    width = xr_ref.shape[2]
    i = pl.program_id(1)

    @pl.when(i == 0)
    def _():
        xtail_ref[...] = jnp.zeros_like(xtail_ref)
        h_ref[...] = jnp.zeros_like(h_ref)

    n_groups = ts // 8
    x = xr_ref[0]
    x3 = x.reshape(n_groups, 8, width)
    tail = xtail_ref[...]
    sub = lax.broadcasted_iota(jnp.int32, (n_groups, 8, width), 1)
    xc3 = cb_ref[...] + cw_ref[RNN_CONV_W - 1:RNN_CONV_W, :] * x3
    for shift in range(1, RNN_CONV_W):
        kk = RNN_CONV_W - 1 - shift
        rot = pltpu.roll(x3, shift, axis=1)
        prev = jnp.concatenate([pltpu.roll(tail, shift, axis=0)[None], rot[:-1]], axis=0)
        xc3 = xc3 + cw_ref[kk:kk + 1, :] * jnp.where(sub < shift, prev, rot)
    xtail_ref[...] = x[ts - 8:ts]
    xc = xc3.reshape(ts, width)

    xcb = xc.astype(BF16)
    r_parts, i_parts = [], []
    for g in range(width // MXU_DIM):
        z = _dot(xcb[:, g * MXU_DIM:(g + 1) * MXU_DIM], wbd_ref[g])
        r_parts.append(z[:, 0:MXU_DIM])
        i_parts.append(z[:, MXU_DIM:2 * MXU_DIM])
    r = jax.nn.sigmoid(jnp.concatenate(r_parts, axis=1) + ba_ref[...])
    ig = jax.nn.sigmoid(jnp.concatenate(i_parts, axis=1) + bx_ref[...])
    log_a = (-RG_C) * r * _softplus(-lam_ref[...])
    a = jnp.exp(log_a)
    m2 = jnp.tanh(-log_a) * (1.0 + a * a)
    mult = jnp.where(m2 > 0.0, m2 * lax.rsqrt(m2), 0.0)
    row = lax.broadcasted_iota(jnp.int32, (ts, width), 0)
    mult = jnp.where(row == jnp.where(i == 0, 0, -1), 1.0, mult)
    b = mult * ig * xc

    a3 = a.reshape(n_groups, 8, width)
    b3 = b.reshape(n_groups, 8, width)
    for d in (1, 2, 4):
        valid = sub >= d
        b3 = jnp.where(valid, a3 * pltpu.roll(b3, d, axis=1), 0.0) + b3
        a3 = jnp.where(valid, a3 * pltpu.roll(a3, d, axis=1), a3)
    h_prev = h_ref[...]
    groups = []
    for g in range(n_groups):
        h_g = a3[g] * h_prev + b3[g]
        groups.append(h_g)
        h_prev = h_g[7:8]
    h_ref[...] = h_prev
    h = jnp.concatenate(groups, axis=0)
    y_ref[0] = (h * _gelu_tanh(gr_ref[0])).astype(y_ref.dtype)


def _rglru(xr, gr, cw, cb, wbd, ba, bx, lam, ts):
    B, S, W = xr.shape
    const = lambda shape: pl.BlockSpec(shape, lambda b, i: (0,) * len(shape))
    row = pl.BlockSpec((1, ts, W), lambda b, i: (b, i, 0))
    return pl.pallas_call(
        _rglru_kernel,
        grid=(B, S // ts),
        in_specs=[row, row, const(cw.shape), const((1, W)), const(wbd.shape),
                  const((1, W)), const((1, W)), const((1, W))],
        out_specs=row,
        out_shape=jax.ShapeDtypeStruct((B, S, W), BF16),
        scratch_shapes=[pltpu.VMEM((8, W), F32), pltpu.VMEM((1, W), F32)],
        compiler_params=pltpu.CompilerParams(
            dimension_semantics=("arbitrary", "arbitrary"), vmem_limit_bytes=VMEM_LIMIT_BYTES),
        name="rg_lru",
    )(xr, gr, cw, cb, wbd, ba, bx, lam)


def _outproj_kernel(ya_ref, yr_ref, x_ref, na_ref, nr_ref, wo_ref, npost_ref, nffn_ref,
                    x1_ref, h2_ref):
    tm, da = ya_ref.shape
    n_parts = 2
    part = tm // n_parts
    for r in range(n_parts):
        rows = pl.ds(r * part, part)
        ya = _rms(ya_ref[rows, :].astype(F32), na_ref[...]).astype(BF16)
        yr = _rms(yr_ref[rows, :].astype(F32), nr_ref[...]).astype(BF16)
        y = _dot(ya, wo_ref[0:da, :]) + _dot(yr, wo_ref[da:, :])
        x1 = x_ref[rows, :] + _rms(y, npost_ref[...])
        x1_ref[rows, :] = x1
        h2_ref[rows, :] = _rms(x1, nffn_ref[...]).astype(BF16)


def _out_projection(ya, yr, x, na, nr, wo, npost, nffn, tm):
    M, D = x.shape
    da, dr = ya.shape[1], yr.shape[1]
    const = lambda shape: pl.BlockSpec(shape, lambda i: (0,) * len(shape),
                                       pipeline_mode=pl.Buffered(1))
    row = lambda width: pl.BlockSpec((tm, width), lambda i: (i, 0))
    return pl.pallas_call(
        _outproj_kernel,
        grid=(M // tm,),
        in_specs=[row(da), row(dr), row(D), const((1, da)), const((1, dr)), const(wo.shape),
                  const((1, D)), const((1, D))],
        out_specs=[row(D), row(D)],
        out_shape=[jax.ShapeDtypeStruct((M, D), F32), jax.ShapeDtypeStruct((M, D), BF16)],
        compiler_params=pltpu.CompilerParams(
            dimension_semantics=("arbitrary",), vmem_limit_bytes=VMEM_LIMIT_BYTES),
        name="out_projection",
    )(ya, yr, x, na, nr, wo, npost, nffn)


def _ffn_kernel(h_ref, wg_ref, wu_ref, cw_ref, cb_ref, wd_ref, x1_ref, gain_ref,
                o_ref, acc_ref, gtail_ref, *, tiles_per_seq):
    tm = h_ref.shape[0]
    tc = wg_ref.shape[1]
    i = pl.program_id(0)
    c = pl.program_id(1)

    @pl.when(c == 0)
    def _():
        acc_ref[...] = jnp.zeros_like(acc_ref)

    h = h_ref[...]
    g = _dot(h, wg_ref[...])
    u = _dot(h, wu_ref[...])
    n_groups = tm // 8
    seq_start = (i % tiles_per_seq) == 0
    tail = jnp.where(seq_start, 0.0, gtail_ref[c])
    gtail_ref[c] = g[tm - 8:tm, :]
    g3 = g.reshape(n_groups, 8, tc)
    sub = lax.broadcasted_iota(jnp.int32, (n_groups, 8, tc), 1)
    conv = cb_ref[...] + cw_ref[FFN_CONV_W - 1:FFN_CONV_W, :] * g3
    for shift in range(1, FFN_CONV_W):
        kk = FFN_CONV_W - 1 - shift
        rot = pltpu.roll(g3, shift, axis=1)
        prev = jnp.concatenate([pltpu.roll(tail, shift, axis=0)[None], rot[:-1]], axis=0)
        conv = conv + cw_ref[kk:kk + 1, :] * jnp.where(sub < shift, prev, rot)
    act = (_gelu_tanh(conv.reshape(tm, tc)) * u).astype(BF16)
    acc_ref[...] += _dot(act, wd_ref[...])

    @pl.when(c == pl.num_programs(1) - 1)
    def _():
        o_ref[...] = x1_ref[...] + _rms(acc_ref[...], gain_ref[...])


def _ffn(h2, wg, wu, cw, cb, wd, x1, gain, tm, tc, tiles_per_seq):
    M, D = h2.shape
    dff = wg.shape[1]
    nc = dff // tc
    return pl.pallas_call(
        functools.partial(_ffn_kernel, tiles_per_seq=tiles_per_seq),
        grid=(M // tm, nc),
        in_specs=[pl.BlockSpec((tm, D), lambda i, c: (i, 0)),
                  pl.BlockSpec((D, tc), lambda i, c: (0, c)),
                  pl.BlockSpec((D, tc), lambda i, c: (0, c)),
                  pl.BlockSpec((FFN_CONV_W, tc), lambda i, c: (0, c)),
                  pl.BlockSpec((1, tc), lambda i, c: (0, c)),
                  pl.BlockSpec((tc, D), lambda i, c: (c, 0)),
                  pl.BlockSpec((tm, D), lambda i, c: (i, 0)),
                  pl.BlockSpec((1, D), lambda i, c: (0, 0))],
        out_specs=pl.BlockSpec((tm, D), lambda i, c: (i, 0)),
        out_shape=jax.ShapeDtypeStruct((M, D), F32),
        scratch_shapes=[pltpu.VMEM((tm, D), F32),
                        pltpu.VMEM((nc, 8, tc), F32)],
        compiler_params=pltpu.CompilerParams(
            dimension_semantics=("arbitrary", "arbitrary"), vmem_limit_bytes=VMEM_LIMIT_BYTES),
        name="conv_gated_mlp",
    )(h2, wg, wu, cw, cb, wd, x1, gain)


def _block_diag(w, group):
    n, r, _ = w.shape
    w = w.reshape(n // group, group, r, r)
    eye = jnp.eye(group, dtype=w.dtype)
    return jnp.einsum("gaij,ab->gaibj", w, eye).reshape(n // group, group * r, group * r)


def _layer(x, norm_mix_pre, w_in, b_forget, conv_rnn_w, conv_rnn_b, w_rg_a, b_rg_a, w_rg_x,
           b_rg_x, rg_lambda, norm_attn_out, norm_rnn_out, w_out, norm_mix_post, norm_ffn_pre,
           w_gate, w_up, conv_ffn_w, conv_ffn_b, w_down, norm_ffn_post):
    B, S, D = x.shape
    d_rnn = conv_rnn_w.shape[1]
    t_seq = min(512, S)

    nh = N_ATTN_HEADS
    w_q, w_k, w_v = w_in[:, 0:D_ATTN], w_in[:, D_ATTN:2 * D_ATTN], w_in[:, 2 * D_ATTN:3 * D_ATTN]
    w_f = w_in[:, 3 * D_ATTN:3 * D_ATTN + nh]
    w_xg = w_in[:, 3 * D_ATTN + nh:]
    w_f_rep = jnp.concatenate([w_f, w_f, w_f, jnp.zeros((D, LANES - 3 * nh), w_in.dtype)], axis=1)
    bf_rep = jnp.concatenate([b_forget, b_forget, b_forget,
                              jnp.zeros((LANES - 3 * nh,), b_forget.dtype)]).reshape(1, LANES)
    group = MXU_DIM // RNN_BLOCK
    wbd = jnp.concatenate([_block_diag(w_rg_a, group), _block_diag(w_rg_x, group)],
                          axis=2).astype(BF16)
    row = lambda v: v.reshape(1, -1)

    hsel = jnp.repeat(jnp.eye(nh, LANES, dtype=BF16), HEAD_DIM, axis=0)
    qt, k, vt, kb, xr, gr, kst = _in_projection(
        x, row(norm_mix_pre), w_q.T.astype(BF16), w_v.T.astype(BF16), w_k.astype(BF16),
        w_xg.astype(BF16), w_f_rep.astype(BF16), bf_rep, hsel, t_seq)
    n_blocks = S // t_seq
    kst = kst[:, :, 0:2, 0:nh].reshape(B, n_blocks, 2, nh // 2, 2).transpose(0, 3, 2, 4, 1)
    kst = jnp.pad(kst.reshape(B, nh // 2, 4, n_blocks), ((0, 0), (0, 0), (0, 4), (0, LANES - n_blocks)))
    y_attn = _attention(qt, k, vt, kb, kst, t_seq)
    y_rnn = _rglru(xr, gr, conv_rnn_w, row(conv_rnn_b), wbd, row(b_rg_a), row(b_rg_x),
                   row(rg_lambda), t_seq)
    M = B * S
    x1, h2 = _out_projection(y_attn.reshape(M, D_ATTN), y_rnn.reshape(M, d_rnn), x.reshape(M, D),
                             row(norm_attn_out), row(norm_rnn_out), w_out.astype(BF16),
                             row(norm_mix_post), row(norm_ffn_pre), t_seq)
    out = _ffn(h2, w_gate.astype(BF16), w_up.astype(BF16), conv_ffn_w, row(conv_ffn_b),
               w_down.astype(BF16), x1, row(norm_ffn_post), t_seq, 1024, S // t_seq)
    return out.reshape(B, S, D)


def kernel(x, norm_mix_pre, w_in, b_forget, conv_rnn_w, conv_rnn_b, w_rg_a, b_rg_a, w_rg_x, b_rg_x, rg_lambda, norm_attn_out, norm_rnn_out, w_out, norm_mix_post, norm_ffn_pre, w_gate, w_up, conv_ffn_w, conv_ffn_b, w_down, norm_ffn_post):
    params = (norm_mix_pre, w_in, b_forget, conv_rnn_w, conv_rnn_b, w_rg_a, b_rg_a, w_rg_x, b_rg_x,
              rg_lambda, norm_attn_out, norm_rnn_out, w_out, norm_mix_post, norm_ffn_pre, w_gate,
              w_up, conv_ffn_w, conv_ffn_b, w_down, norm_ffn_post)
    for l in range(norm_mix_pre.shape[0]):
        x = _layer(x, *(p[l] for p in params))
    return x
```

```python
import functools
import math

import jax
import jax.numpy as jnp
from jax import lax
from jax.experimental import pallas as pl
from jax.experimental.pallas import tpu as pltpu

N_ATTN_HEADS = 16
HEAD_DIM = 64
D_ATTN = N_ATTN_HEADS * HEAD_DIM
N_RNN_BLOCKS = 16
RNN_BLOCK = 64
RNN_CONV_W = 4
RG_C = 8.0
FFN_CONV_W = 3
EPS = 1e-6
LOG2E = 1.4426950408889634

ZERO_EXP2 = -150.0
BOUND_SLACK = 8.0
NORM_SLACK = 1.05

LANES = 128
MXU_DIM = 256
VMEM_LIMIT_BYTES = 58 * 1024 * 1024

F32 = jnp.float32
BF16 = jnp.bfloat16


def _rms(x, gain):
    return x * lax.rsqrt(jnp.mean(x * x, axis=-1, keepdims=True) + EPS) * gain


def _gelu_tanh(x):
    cdf = 0.5 * (1.0 + jnp.tanh(math.sqrt(2.0 / math.pi) * (x + 0.044715 * (x * x * x))))
    return x * cdf


def _softplus(x):
    return jnp.maximum(x, 0.0) + jnp.log1p(jnp.exp(-jnp.abs(x)))


def _split3(x):
    hi = x.astype(BF16).astype(F32)
    r = x - hi
    mid = r.astype(BF16).astype(F32)
    lo = (r - mid).astype(BF16).astype(F32)
    return hi, mid, lo


def _dot(a, b):
    return jnp.dot(a, b, preferred_element_type=F32)


def _dot_nt(a, b):
    return lax.dot_general(a, b, (((1,), (1,)), ((), ())), preferred_element_type=F32)


def _proj_kernel(x_ref, g_ref, wqt_ref, wvt_ref, wk_ref, wxg_ref, wf_ref, bf_ref, hsel_ref,
                 cw_ref, cb_ref, wbd_ref, ba_ref, bx_ref, lam_ref,
                 qt_ref, k_ref, vt_ref, kb_ref, yr_ref, kst_ref, carry_ref, xtail_ref, hstate_ref):
    tm = x_ref.shape[1]
    first_tile = pl.program_id(1) == 0

    @pl.when(first_tile)
    def _():
        carry_ref[...] = jnp.zeros_like(carry_ref)
        xtail_ref[...] = jnp.zeros_like(xtail_ref)
        hstate_ref[...] = jnp.zeros_like(hstate_ref)

    h = _rms(x_ref[0], g_ref[...]).astype(BF16)
    lane = lax.broadcasted_iota(jnp.int32, (tm, LANES), 1)

    fl = _dot(h, wf_ref[...]) + bf_ref[...]
    logf = jnp.minimum(fl, 0.0) - jnp.log1p(jnp.exp(-jnp.abs(fl)))
    logf = jnp.where(lane < 3 * N_ATTN_HEADS, logf, 0.0)
    hi, mid, lo = _split3(logf)
    pieces = jnp.concatenate([hi, mid, lo], axis=1).astype(BF16)

    xr = _dot(h, wxg_ref[:, 0:D_ATTN])
    gr = _dot(h, wxg_ref[:, D_ATTN:2 * D_ATTN])
    xc = _rglru_conv(xr, cw_ref, cb_ref, xtail_ref)
    yr_ref[0] = _rglru_recurrence(xc, gr, first_tile, wbd_ref, ba_ref, bx_ref, lam_ref,
                                  hstate_ref).astype(yr_ref.dtype)

    qt_ref[0] = (_dot_nt(wqt_ref[...], h) * (HEAD_DIM ** -0.5 * LOG2E)).astype(BF16)

    rows = lax.broadcasted_iota(jnp.int32, (tm, tm), 0)
    cols = lax.broadcasted_iota(jnp.int32, (tm, tm), 1)
    tri = jnp.where(rows >= cols, 1.0, 0.0).astype(BF16)
    cs = _dot(tri, pieces)
    c = cs[:, 0:LANES] + cs[:, LANES:2 * LANES] + cs[:, 2 * LANES:3 * LANES] + carry_ref[...]
    carry_ref[...] = c[tm - 1:tm, :]
    bias = c * (-LOG2E)
    bhi, bmid, blo = _split3(bias)
    kb = jnp.where(lane < N_ATTN_HEADS, bhi, jnp.where(lane < 2 * N_ATTN_HEADS, bmid, blo))
    kb_ref[0] = kb.astype(BF16)

    vt_ref[0] = _dot_nt(wvt_ref[...], h).astype(BF16)
    k = _dot(h, wk_ref[...]).astype(BF16)
    k_ref[0] = k
    k2 = k.astype(F32)
    kn2 = _dot((k2 * k2).astype(BF16), hsel_ref[...])
    kst_ref[0, 0] = jnp.concatenate(
        [jnp.max(kn2, axis=0, keepdims=True), jnp.max(bias, axis=0, keepdims=True),
         jnp.zeros((6, LANES), F32)], axis=0)


def _in_projection(x, gain, wqt, wvt, wk, wxg, wf, bf_rep, hsel, cw, cb, wbd, ba, bx, lam, tm):
    B, S, D = x.shape
    const = lambda a: pl.BlockSpec(a.shape, lambda b, i: (0,) * a.ndim, pipeline_mode=pl.Buffered(1))
    row = lambda width: pl.BlockSpec((1, tm, width), lambda b, i: (b, i, 0))
    col = pl.BlockSpec((1, D_ATTN, tm), lambda b, i: (b, 0, i))
    weights = (gain, wqt, wvt, wk, wxg, wf, bf_rep, hsel, cw, cb, wbd, ba, bx, lam)
    d_rnn = cw.shape[1]
    return pl.pallas_call(
        _proj_kernel,
        grid=(B, S // tm),
        in_specs=[row(D)] + [const(w) for w in weights],
        out_specs=[col, row(D_ATTN), col, row(LANES), row(d_rnn),
                   pl.BlockSpec((1, 1, 8, LANES), lambda b, i: (b, i, 0, 0))],
        out_shape=[jax.ShapeDtypeStruct((B, D_ATTN, S), BF16),
                   jax.ShapeDtypeStruct((B, S, D_ATTN), BF16),
                   jax.ShapeDtypeStruct((B, D_ATTN, S), BF16),
                   jax.ShapeDtypeStruct((B, S, LANES), BF16),
                   jax.ShapeDtypeStruct((B, S, d_rnn), BF16),
                   jax.ShapeDtypeStruct((B, S // tm, 8, LANES), F32)],
        scratch_shapes=[pltpu.VMEM((1, LANES), F32),
                        pltpu.VMEM((8, d_rnn), F32),
                        pltpu.VMEM((1, d_rnn), F32)],
        compiler_params=pltpu.CompilerParams(
            dimension_semantics=("arbitrary", "arbitrary"), vmem_limit_bytes=VMEM_LIMIT_BYTES),
        name="in_projection",
    )(x, *weights)


V_ROWS = HEAD_DIM + 16


def _attn_kernel(qt_ref, k_ref, vt_ref, kb_ref, kst_ref, o_ref,
                 rhs_ref, m_ref, acc_ref, s_a_ref, s_b_ref, s_c_ref, mb_a_ref, mb_b_ref, mb_c_ref,
                 seq_ref, *, tq):
    hp = pl.program_id(1)
    tk = tq
    n_kblocks = k_ref.shape[1] // tk
    n_qblocks = qt_ref.shape[2] // tq
    ones = jnp.ones((V_ROWS - HEAD_DIM, tk), BF16)
    slot_a = (s_a_ref, mb_a_ref)
    slot_b = (s_b_ref, mb_b_ref)
    slot_c = (s_c_ref, mb_c_ref)

    def q_block(i):
        return qt_ref[0, :, pl.ds(pl.multiple_of(i * tq, tq), tq)]

    def scores(j, slot, diagonal, hh):
        s_ref, mb_ref = slot
        start = pl.multiple_of(j * tk, tk)
        lhs = jnp.concatenate([k_ref[0, pl.ds(start, tk), :], kb_ref[0, pl.ds(start, tk), :]], axis=1)
        s = _dot(lhs, rhs_ref[hh])
        if diagonal:
            key = lax.broadcasted_iota(jnp.int32, (tk, tq), 0)
            qry = lax.broadcasted_iota(jnp.int32, (tk, tq), 1)
            s = jnp.where(key <= qry, s, -jnp.inf)
        s_ref[hh] = s
        mb_ref[hh] = jnp.max(s, axis=0, keepdims=True)

    def consume(j, slot, hh):
        s_ref, mb_ref = slot
        start = pl.multiple_of(j * tk, tk)
        m_old = m_ref[hh]
        m_new = jnp.maximum(m_old, mb_ref[hh])
        p = jnp.exp2(s_ref[hh] - m_new).astype(BF16)
        vt = jnp.concatenate(
            [vt_ref[0, hh * HEAD_DIM:(hh + 1) * HEAD_DIM, pl.ds(start, tk)], ones], axis=0)
        acc_ref[hh] = acc_ref[hh] * jnp.exp2(m_old - m_new) + _dot(vt, p)
        m_ref[hh] = m_new

    def stage(js, slot_s, jc, slot_prev):
        for hh in range(2):
            scores(js, slot_s, False, hh)
            consume(jc, slot_prev, hh)

    def fill(i):
        qt = q_block(i)
        row = lax.broadcasted_iota(jnp.int32, (LANES, tq), 0)
        for hh in range(2):
            head = 2 * hp + hh
            qh = jnp.where((row >= hh * HEAD_DIM) & (row < (hh + 1) * HEAD_DIM), qt, jnp.zeros_like(qt))
            sel = (row == head) | (row == head + N_ATTN_HEADS) | (row == head + 2 * N_ATTN_HEADS)
            rhs_ref[hh, 0:LANES, :] = qh
            rhs_ref[hh, LANES:2 * LANES, :] = jnp.where(sel, 1.0, 0.0).astype(BF16)
        for hh in range(2):
            scores(i, slot_c, True, hh)

    def plan(i):
        qt = q_block(i)
        lane = lax.broadcasted_iota(jnp.int32, (1, LANES), 1)
        need = jnp.zeros((1, LANES), jnp.int32)
        for hh in range(2):
            qf = qt[hh * HEAD_DIM:(hh + 1) * HEAD_DIM, :].astype(F32)
            qn2 = jnp.max(jnp.sum(qf * qf, axis=0, keepdims=True), axis=1, keepdims=True)
            m_min = jnp.min(mb_c_ref[hh], axis=1, keepdims=True)
            kn2 = kst_ref[0, 0, hh:hh + 1, :]
            bias_max = kst_ref[0, 0, 2 + hh:3 + hh, :]
            bound = jnp.sqrt(qn2 * kn2 * NORM_SLACK) + bias_max - m_min
            need = need | jnp.where(bound >= ZERO_EXP2 - BOUND_SLACK, 1, 0)
        need = jnp.where(lane < i, need, 0)
        seq_ref[0] = i
        count = jnp.int32(1)
        for j in range(n_kblocks - 2, -1, -1):
            seq_ref[count] = j
            count = count + need[0, j]
        return count - 1

    def finish(i, last, slot):
        for hh in range(2):
            consume(seq_ref[last], slot, hh)
        outs = []
        for hh in range(2):
            acc = acc_ref[hh]
            outs.append(acc[0:HEAD_DIM] * (1.0 / acc[HEAD_DIM:HEAD_DIM + 1]))
        out = jnp.concatenate(outs, axis=0).T
        o_ref[0, pl.ds(pl.multiple_of(i * tq, tq), tq), :] = out.astype(o_ref.dtype)
        fill(jnp.minimum(i + 1, n_qblocks - 1))

    def query_block(i, carry):
        last = plan(i)
        m_ref[...] = jnp.full_like(m_ref, -jnp.inf)
        acc_ref[...] = jnp.zeros_like(acc_ref)

        @pl.when(last == 0)
        def _():
            finish(i, last, slot_c)

        @pl.when(last > 0)
        def _():
            stage(seq_ref[1], slot_b, seq_ref[0], slot_c)
            n_loop = (last - 1) // 2

            def body(t, carry):
                n = 1 + 2 * t
                stage(seq_ref[n + 1], slot_a, seq_ref[n], slot_b)
                stage(seq_ref[n + 2], slot_b, seq_ref[n + 1], slot_a)
                return carry

            lax.fori_loop(0, n_loop, body, 0)
            n = 1 + 2 * n_loop

            @pl.when(last == n)
            def _():
                finish(i, last, slot_b)

            @pl.when(last == n + 1)
            def _():
                stage(seq_ref[last], slot_a, seq_ref[n], slot_b)
                finish(i, last, slot_a)

        return carry

    fill(0)
    lax.fori_loop(0, n_qblocks, query_block, 0)


def _attention(qt, k, vt, kb, kst, tq):
    B, S, _ = k.shape
    n_pairs = N_ATTN_HEADS // 2
    rows = pl.BlockSpec((1, S, LANES), lambda b, p: (b, 0, p))
    cols = pl.BlockSpec((1, LANES, S), lambda b, p: (b, p, 0))
    return pl.pallas_call(
        functools.partial(_attn_kernel, tq=tq),
        grid=(B, n_pairs),
        in_specs=[cols, rows, cols, pl.BlockSpec((1, S, LANES), lambda b, p: (b, 0, 0)),
                  pl.BlockSpec((1, 1, 8, LANES), lambda b, p: (b, p, 0, 0))],
        out_specs=rows,
        out_shape=jax.ShapeDtypeStruct((B, S, D_ATTN), BF16),
        scratch_shapes=[pltpu.VMEM((2, 2 * LANES, tq), BF16),
                        pltpu.VMEM((2, 1, tq), F32),
                        pltpu.VMEM((2, V_ROWS, tq), F32),
                        pltpu.VMEM((2, tq, tq), F32),
                        pltpu.VMEM((2, tq, tq), F32),
                        pltpu.VMEM((2, tq, tq), F32),
                        pltpu.VMEM((2, 1, tq), F32),
                        pltpu.VMEM((2, 1, tq), F32),
                        pltpu.VMEM((2, 1, tq), F32),
                        pltpu.SMEM((S // tq + 1,), jnp.int32)],
        compiler_params=pltpu.CompilerParams(
            dimension_semantics=("arbitrary", "arbitrary"), vmem_limit_bytes=VMEM_LIMIT_BYTES),
        name="fox_attention",
    )(qt, k, vt, kb, kst)


def _rglru_conv(x, cw_ref, cb_ref, xtail_ref):
    ts, width = x.shape
    n_groups = ts // 8
    x3 = x.reshape(n_groups, 8, width)
    tail = xtail_ref[...]
    sub = lax.broadcasted_iota(jnp.int32, (n_groups, 8, width), 1)
    xc3 = cb_ref[...] + cw_ref[RNN_CONV_W - 1:RNN_CONV_W, :] * x3
    for shift in range(1, RNN_CONV_W):
        kk = RNN_CONV_W - 1 - shift
        rot = pltpu.roll(x3, shift, axis=1)
        prev = jnp.concatenate([pltpu.roll(tail, shift, axis=0)[None], rot[:-1]], axis=0)
        xc3 = xc3 + cw_ref[kk:kk + 1, :] * jnp.where(sub < shift, prev, rot)
    xtail_ref[...] = x[ts - 8:ts]
    return xc3.reshape(ts, width)


def _rglru_recurrence(xc, gate, first_tile, wbd_ref, ba_ref, bx_ref, lam_ref, h_ref):
    ts, width = xc.shape
    n_groups = ts // 8
    sub = lax.broadcasted_iota(jnp.int32, (n_groups, 8, width), 1)
    xcb = xc.astype(BF16)
    r_parts, i_parts = [], []
    for g in range(width // MXU_DIM):
        z = _dot(xcb[:, g * MXU_DIM:(g + 1) * MXU_DIM], wbd_ref[g])
        r_parts.append(z[:, 0:MXU_DIM])
        i_parts.append(z[:, MXU_DIM:2 * MXU_DIM])
    r = jax.nn.sigmoid(jnp.concatenate(r_parts, axis=1) + ba_ref[...])
    ig = jax.nn.sigmoid(jnp.concatenate(i_parts, axis=1) + bx_ref[...])
    log_a = (-RG_C) * r * _softplus(-lam_ref[...])
    a = jnp.exp(log_a)
    m2 = jnp.tanh(-log_a) * (1.0 + a * a)
    mult = jnp.where(m2 > 0.0, m2 * lax.rsqrt(m2), 0.0)
    row = lax.broadcasted_iota(jnp.int32, (ts, width), 0)
    mult = jnp.where(row == jnp.where(first_tile, 0, -1), 1.0, mult)
    b = mult * ig * xc

    a3 = a.reshape(n_groups, 8, width)
    b3 = b.reshape(n_groups, 8, width)
    for d in (1, 2, 4):
        valid = sub >= d
        b3 = jnp.where(valid, a3 * pltpu.roll(b3, d, axis=1), 0.0) + b3
        a3 = jnp.where(valid, a3 * pltpu.roll(a3, d, axis=1), a3)
    h_prev = h_ref[...]
    groups = []
    for g in range(n_groups):
        h_g = a3[g] * h_prev + b3[g]
        groups.append(h_g)
        h_prev = h_g[7:8]
    h_ref[...] = h_prev
    return jnp.concatenate(groups, axis=0) * _gelu_tanh(gate)


def _outproj_kernel(ya_ref, yr_ref, x_ref, na_ref, nr_ref, wo_ref, npost_ref, nffn_ref,
                    x1_ref, h2_ref):
    tm, da = ya_ref.shape
    n_parts = 2
    part = tm // n_parts
    for r in range(n_parts):
        rows = pl.ds(r * part, part)
        ya = _rms(ya_ref[rows, :].astype(F32), na_ref[...]).astype(BF16)
        yr = _rms(yr_ref[rows, :].astype(F32), nr_ref[...]).astype(BF16)
        y = _dot(ya, wo_ref[0:da, :]) + _dot(yr, wo_ref[da:, :])
        x1 = x_ref[rows, :] + _rms(y, npost_ref[...])
        x1_ref[rows, :] = x1
        h2_ref[rows, :] = _rms(x1, nffn_ref[...]).astype(BF16)


def _out_projection(ya, yr, x, na, nr, wo, npost, nffn, tm):
    M, D = x.shape
    da, dr = ya.shape[1], yr.shape[1]
    const = lambda shape: pl.BlockSpec(shape, lambda i: (0,) * len(shape),
                                       pipeline_mode=pl.Buffered(1))
    row = lambda width: pl.BlockSpec((tm, width), lambda i: (i, 0))
    return pl.pallas_call(
        _outproj_kernel,
        grid=(M // tm,),
        in_specs=[row(da), row(dr), row(D), const((1, da)), const((1, dr)), const(wo.shape),
                  const((1, D)), const((1, D))],
        out_specs=[row(D), row(D)],
        out_shape=[jax.ShapeDtypeStruct((M, D), F32), jax.ShapeDtypeStruct((M, D), BF16)],
        compiler_params=pltpu.CompilerParams(
            dimension_semantics=("arbitrary",), vmem_limit_bytes=VMEM_LIMIT_BYTES),
        name="out_projection",
    )(ya, yr, x, na, nr, wo, npost, nffn)


def _ffn_kernel(h_ref, wg_ref, wu_ref, cw_ref, cb_ref, wd_ref, x1_ref, gain_ref,
                o_ref, acc_ref, gtail_ref, *, tiles_per_seq):
    tm = h_ref.shape[0]
    tc = wg_ref.shape[1]
    i = pl.program_id(0)
    c = pl.program_id(1)

    @pl.when(c == 0)
    def _():
        acc_ref[...] = jnp.zeros_like(acc_ref)

    h = h_ref[...]
    g = _dot(h, wg_ref[...])
    u = _dot(h, wu_ref[...])
    n_groups = tm // 8
    seq_start = (i % tiles_per_seq) == 0
    tail = jnp.where(seq_start, 0.0, gtail_ref[c])
    gtail_ref[c] = g[tm - 8:tm, :]
    g3 = g.reshape(n_groups, 8, tc)
    sub = lax.broadcasted_iota(jnp.int32, (n_groups, 8, tc), 1)
    conv = cb_ref[...] + cw_ref[FFN_CONV_W - 1:FFN_CONV_W, :] * g3
    for shift in range(1, FFN_CONV_W):
        kk = FFN_CONV_W - 1 - shift
        rot = pltpu.roll(g3, shift, axis=1)
        prev = jnp.concatenate([pltpu.roll(tail, shift, axis=0)[None], rot[:-1]], axis=0)
        conv = conv + cw_ref[kk:kk + 1, :] * jnp.where(sub < shift, prev, rot)
    act = (_gelu_tanh(conv.reshape(tm, tc)) * u).astype(BF16)
    acc_ref[...] += _dot(act, wd_ref[...])

    @pl.when(c == pl.num_programs(1) - 1)
    def _():
        o_ref[...] = x1_ref[...] + _rms(acc_ref[...], gain_ref[...])


def _ffn(h2, wg, wu, cw, cb, wd, x1, gain, tm, tc, tiles_per_seq):
    M, D = h2.shape
    dff = wg.shape[1]
    nc = dff // tc
    return pl.pallas_call(
        functools.partial(_ffn_kernel, tiles_per_seq=tiles_per_seq),
        grid=(M // tm, nc),
        in_specs=[pl.BlockSpec((tm, D), lambda i, c: (i, 0)),
                  pl.BlockSpec((D, tc), lambda i, c: (0, c)),
                  pl.BlockSpec((D, tc), lambda i, c: (0, c)),
                  pl.BlockSpec((FFN_CONV_W, tc), lambda i, c: (0, c)),
                  pl.BlockSpec((1, tc), lambda i, c: (0, c)),
                  pl.BlockSpec((tc, D), lambda i, c: (c, 0)),
                  pl.BlockSpec((tm, D), lambda i, c: (i, 0)),
                  pl.BlockSpec((1, D), lambda i, c: (0, 0))],
        out_specs=pl.BlockSpec((tm, D), lambda i, c: (i, 0)),
        out_shape=jax.ShapeDtypeStruct((M, D), F32),
        scratch_shapes=[pltpu.VMEM((tm, D), F32),
                        pltpu.VMEM((nc, 8, tc), F32)],
        compiler_params=pltpu.CompilerParams(
            dimension_semantics=("arbitrary", "arbitrary"), vmem_limit_bytes=VMEM_LIMIT_BYTES),
        name="conv_gated_mlp",
    )(h2, wg, wu, cw, cb, wd, x1, gain)


def _block_diag(w, group):
    n, r, _ = w.shape
    w = w.reshape(n // group, group, r, r)
    eye = jnp.eye(group, dtype=w.dtype)
    return jnp.einsum("gaij,ab->gaibj", w, eye).reshape(n // group, group * r, group * r)


def _layer(x, norm_mix_pre, w_in, b_forget, conv_rnn_w, conv_rnn_b, w_rg_a, b_rg_a, w_rg_x,
           b_rg_x, rg_lambda, norm_attn_out, norm_rnn_out, w_out, norm_mix_post, norm_ffn_pre,
           w_gate, w_up, conv_ffn_w, conv_ffn_b, w_down, norm_ffn_post):
    B, S, D = x.shape
    d_rnn = conv_rnn_w.shape[1]
    t_seq = min(512, S)

    nh = N_ATTN_HEADS
    w_q, w_k, w_v = w_in[:, 0:D_ATTN], w_in[:, D_ATTN:2 * D_ATTN], w_in[:, 2 * D_ATTN:3 * D_ATTN]
    w_f = w_in[:, 3 * D_ATTN:3 * D_ATTN + nh]
    w_xg = w_in[:, 3 * D_ATTN + nh:]
    w_f_rep = jnp.concatenate([w_f, w_f, w_f, jnp.zeros((D, LANES - 3 * nh), w_in.dtype)], axis=1)
    bf_rep = jnp.concatenate([b_forget, b_forget, b_forget,
                              jnp.zeros((LANES - 3 * nh,), b_forget.dtype)]).reshape(1, LANES)
    group = MXU_DIM // RNN_BLOCK
    wbd = jnp.concatenate([_block_diag(w_rg_a, group), _block_diag(w_rg_x, group)],
                          axis=2).astype(BF16)
    row = lambda v: v.reshape(1, -1)

    hsel = jnp.repeat(jnp.eye(nh, LANES, dtype=BF16), HEAD_DIM, axis=0)
    qt, k, vt, kb, y_rnn, kst = _in_projection(
        x, row(norm_mix_pre), w_q.T.astype(BF16), w_v.T.astype(BF16), w_k.astype(BF16),
        w_xg.astype(BF16), w_f_rep.astype(BF16), bf_rep, hsel,
        conv_rnn_w, row(conv_rnn_b), wbd, row(b_rg_a), row(b_rg_x), row(rg_lambda), t_seq)
    n_blocks = S // t_seq
    kst = kst[:, :, 0:2, 0:nh].reshape(B, n_blocks, 2, nh // 2, 2).transpose(0, 3, 2, 4, 1)
    kst = jnp.pad(kst.reshape(B, nh // 2, 4, n_blocks), ((0, 0), (0, 0), (0, 4), (0, LANES - n_blocks)))
    y_attn = _attention(qt, k, vt, kb, kst, t_seq)
    M = B * S
    x1, h2 = _out_projection(y_attn.reshape(M, D_ATTN), y_rnn.reshape(M, d_rnn), x.reshape(M, D),
                             row(norm_attn_out), row(norm_rnn_out), w_out.astype(BF16),
                             row(norm_mix_post), row(norm_ffn_pre), t_seq)
    out = _ffn(h2, w_gate.astype(BF16), w_up.astype(BF16), conv_ffn_w, row(conv_ffn_b),
               w_down.astype(BF16), x1, row(norm_ffn_post), t_seq, 1024, S // t_seq)
    return out.reshape(B, S, D)


def kernel(x, norm_mix_pre, w_in, b_forget, conv_rnn_w, conv_rnn_b, w_rg_a, b_rg_a, w_rg_x, b_rg_x, rg_lambda, norm_attn_out, norm_rnn_out, w_out, norm_mix_post, norm_ffn_pre, w_gate, w_up, conv_ffn_w, conv_ffn_b, w_down, norm_ffn_post):
    params = (norm_mix_pre, w_in, b_forget, conv_rnn_w, conv_rnn_b, w_rg_a, b_rg_a, w_rg_x, b_rg_x,
              rg_lambda, norm_attn_out, norm_rnn_out, w_out, norm_mix_post, norm_ffn_pre, w_gate,
              w_up, conv_ffn_w, conv_ffn_b, w_down, norm_ffn_post)
    for l in range(norm_mix_pre.shape[0]):
        x = _layer(x, *(p[l] for p in params))
    return x
```

```python
import functools
import math

import jax
import jax.numpy as jnp
from jax import lax
from jax.experimental import pallas as pl
from jax.experimental.pallas import tpu as pltpu

N_ATTN_HEADS = 16
HEAD_DIM = 64
D_ATTN = N_ATTN_HEADS * HEAD_DIM
N_RNN_BLOCKS = 16
RNN_BLOCK = 64
RNN_CONV_W = 4
RG_C = 8.0
FFN_CONV_W = 3
EPS = 1e-6
LOG2E = 1.4426950408889634

ZERO_EXP2 = -float("inf")
BOUND_SLACK = 8.0
NORM_SLACK = 1.05

LANES = 128
MXU_DIM = 256
VMEM_LIMIT_BYTES = 58 * 1024 * 1024
TOKEN_TILE = 512
MLP_CHUNK = 1024

F32 = jnp.float32
BF16 = jnp.bfloat16


def _rms(x, gain):
    return x * lax.rsqrt(jnp.mean(x * x, axis=-1, keepdims=True) + EPS) * gain


def _gelu_tanh(x):
    cdf = 0.5 * (1.0 + jnp.tanh(math.sqrt(2.0 / math.pi) * (x + 0.044715 * (x * x * x))))
    return x * cdf


def _softplus(x):
    return jnp.maximum(x, 0.0) + jnp.log1p(jnp.exp(-jnp.abs(x)))


def _split3(x):
    hi = x.astype(BF16).astype(F32)
    r = x - hi
    mid = r.astype(BF16).astype(F32)
    lo = (r - mid).astype(BF16).astype(F32)
    return hi, mid, lo


def _dot(a, b):
    return jnp.dot(a, b, preferred_element_type=F32)


def _dot_nt(a, b):
    return lax.dot_general(a, b, (((1,), (1,)), ((), ())), preferred_element_type=F32)


def _proj_kernel(x_ref, g_ref, wqt_ref, wvt_ref, wk_ref, wxg_ref, wf_ref, bf_ref, hsel_ref,
                 cw_ref, cb_ref, wbd_ref, ba_ref, bx_ref, lam_ref,
                 qt_ref, k_ref, vt_ref, kb_ref, yr_ref, kst_ref, carry_ref, xtail_ref, hstate_ref):
    tm = x_ref.shape[1]
    first_tile = pl.program_id(1) == 0

    @pl.when(first_tile)
    def _():
        carry_ref[...] = jnp.zeros_like(carry_ref)
        xtail_ref[...] = jnp.zeros_like(xtail_ref)
        hstate_ref[...] = jnp.zeros_like(hstate_ref)

    h = _rms(x_ref[0], g_ref[...]).astype(BF16)
    lane = lax.broadcasted_iota(jnp.int32, (tm, LANES), 1)

    fl = _dot(h, wf_ref[...]) + bf_ref[...]
    logf = jnp.minimum(fl, 0.0) - jnp.log1p(jnp.exp(-jnp.abs(fl)))
    logf = jnp.where(lane < 3 * N_ATTN_HEADS, logf, 0.0)
    hi, mid, lo = _split3(logf)
    pieces = jnp.concatenate([hi, mid, lo], axis=1).astype(BF16)

    xr = _dot(h, wxg_ref[:, 0:D_ATTN])
    gr = _dot(h, wxg_ref[:, D_ATTN:2 * D_ATTN])
    xc = _rglru_conv(xr, cw_ref, cb_ref, xtail_ref)
    yr_ref[0] = _rglru_recurrence(xc, gr, first_tile, wbd_ref, ba_ref, bx_ref, lam_ref,
                                  hstate_ref).astype(yr_ref.dtype)

    qt_ref[0] = (_dot_nt(wqt_ref[...], h) * (HEAD_DIM ** -0.5 * LOG2E)).astype(BF16)

    rows = lax.broadcasted_iota(jnp.int32, (tm, tm), 0)
    cols = lax.broadcasted_iota(jnp.int32, (tm, tm), 1)
    tri = jnp.where(rows >= cols, 1.0, 0.0).astype(BF16)
    cs = _dot(tri, pieces)
    c = cs[:, 0:LANES] + cs[:, LANES:2 * LANES] + cs[:, 2 * LANES:3 * LANES] + carry_ref[...]
    carry_ref[...] = c[tm - 1:tm, :]
    bias = c * (-LOG2E)
    bhi, bmid, blo = _split3(bias)
    kb = jnp.where(lane < N_ATTN_HEADS, bhi, jnp.where(lane < 2 * N_ATTN_HEADS, bmid, blo))
    kb_ref[0] = kb.astype(BF16)

    vt_ref[0] = _dot_nt(wvt_ref[...], h).astype(BF16)
    k = _dot(h, wk_ref[...]).astype(BF16)
    k_ref[0] = k
    k2 = k.astype(F32)
    kn2 = _dot((k2 * k2).astype(BF16), hsel_ref[...])
    kst_ref[0, 0] = jnp.concatenate(
        [jnp.max(kn2, axis=0, keepdims=True), jnp.max(bias, axis=0, keepdims=True),
         jnp.zeros((6, LANES), F32)], axis=0)


def _in_projection(x, gain, wqt, wvt, wk, wxg, wf, bf_rep, hsel, cw, cb, wbd, ba, bx, lam, tm):
    B, S, D = x.shape
    const = lambda a: pl.BlockSpec(a.shape, lambda b, i: (0,) * a.ndim, pipeline_mode=pl.Buffered(1))
    row = lambda width: pl.BlockSpec((1, tm, width), lambda b, i: (b, i, 0))
    col = pl.BlockSpec((1, D_ATTN, tm), lambda b, i: (b, 0, i))
    weights = (gain, wqt, wvt, wk, wxg, wf, bf_rep, hsel, cw, cb, wbd, ba, bx, lam)
    d_rnn = cw.shape[1]
    return pl.pallas_call(
        _proj_kernel,
        grid=(B, S // tm),
        in_specs=[row(D)] + [const(w) for w in weights],
        out_specs=[col, row(D_ATTN), col, row(LANES), row(d_rnn),
                   pl.BlockSpec((1, 1, 8, LANES), lambda b, i: (b, i, 0, 0))],
        out_shape=[jax.ShapeDtypeStruct((B, D_ATTN, S), BF16),
                   jax.ShapeDtypeStruct((B, S, D_ATTN), BF16),
                   jax.ShapeDtypeStruct((B, D_ATTN, S), BF16),
                   jax.ShapeDtypeStruct((B, S, LANES), BF16),
                   jax.ShapeDtypeStruct((B, S, d_rnn), BF16),
                   jax.ShapeDtypeStruct((B, S // tm, 8, LANES), F32)],
        scratch_shapes=[pltpu.VMEM((1, LANES), F32),
                        pltpu.VMEM((8, d_rnn), F32),
                        pltpu.VMEM((1, d_rnn), F32)],
        compiler_params=pltpu.CompilerParams(
            dimension_semantics=("arbitrary", "arbitrary"), vmem_limit_bytes=VMEM_LIMIT_BYTES),
        name="in_projection",
    )(x, *weights)


V_ROWS = HEAD_DIM + 16


def _attn_kernel(qt_ref, k_ref, vt_ref, kb_ref, kst_ref, o_ref,
                 rhs_ref, m_ref, acc_ref, s_a_ref, s_b_ref, s_c_ref, mb_a_ref, mb_b_ref, mb_c_ref,
                 seq_ref, *, tq):
    hp = pl.program_id(1)
    tk = tq
    n_kblocks = k_ref.shape[1] // tk
    n_qblocks = qt_ref.shape[2] // tq
    ones = jnp.ones((V_ROWS - HEAD_DIM, tk), BF16)
    slot_a = (s_a_ref, mb_a_ref)
    slot_b = (s_b_ref, mb_b_ref)
    slot_c = (s_c_ref, mb_c_ref)

    def q_block(i):
        return qt_ref[0, :, pl.ds(pl.multiple_of(i * tq, tq), tq)]

    def scores(j, slot, diagonal, hh):
        s_ref, mb_ref = slot
        start = pl.multiple_of(j * tk, tk)
        lhs = jnp.concatenate([k_ref[0, pl.ds(start, tk), :], kb_ref[0, pl.ds(start, tk), :]], axis=1)
        s = _dot(lhs, rhs_ref[hh])
        if diagonal:
            key = lax.broadcasted_iota(jnp.int32, (tk, tq), 0)
            qry = lax.broadcasted_iota(jnp.int32, (tk, tq), 1)
            s = jnp.where(key <= qry, s, -jnp.inf)
        s_ref[hh] = s
        mb_ref[hh] = jnp.max(s, axis=0, keepdims=True)

    def consume(j, slot, hh):
        s_ref, mb_ref = slot
        start = pl.multiple_of(j * tk, tk)
        m_old = m_ref[hh]
        m_new = jnp.maximum(m_old, mb_ref[hh])
        p = jnp.exp2(s_ref[hh] - m_new).astype(BF16)
        vt = jnp.concatenate(
            [vt_ref[0, hh * HEAD_DIM:(hh + 1) * HEAD_DIM, pl.ds(start, tk)], ones], axis=0)
        acc_ref[hh] = acc_ref[hh] * jnp.exp2(m_old - m_new) + _dot(vt, p)
        m_ref[hh] = m_new

    def stage(js, slot_s, jc, slot_prev):
        for hh in range(2):
            scores(js, slot_s, False, hh)
            consume(jc, slot_prev, hh)

    def fill(i):
        qt = q_block(i)
        row = lax.broadcasted_iota(jnp.int32, (LANES, tq), 0)
        for hh in range(2):
            head = 2 * hp + hh
            qh = jnp.where((row >= hh * HEAD_DIM) & (row < (hh + 1) * HEAD_DIM), qt, jnp.zeros_like(qt))
            sel = (row == head) | (row == head + N_ATTN_HEADS) | (row == head + 2 * N_ATTN_HEADS)
            rhs_ref[hh, 0:LANES, :] = qh
            rhs_ref[hh, LANES:2 * LANES, :] = jnp.where(sel, 1.0, 0.0).astype(BF16)
        for hh in range(2):
            scores(i, slot_c, True, hh)

    def plan(i):
        qt = q_block(i)
        lane = lax.broadcasted_iota(jnp.int32, (1, LANES), 1)
        need = jnp.zeros((1, LANES), jnp.int32)
        for hh in range(2):
            qf = qt[hh * HEAD_DIM:(hh + 1) * HEAD_DIM, :].astype(F32)
            qn2 = jnp.max(jnp.sum(qf * qf, axis=0, keepdims=True), axis=1, keepdims=True)
            m_min = jnp.min(mb_c_ref[hh], axis=1, keepdims=True)
            kn2 = kst_ref[0, 0, hh:hh + 1, :]
            bias_max = kst_ref[0, 0, 2 + hh:3 + hh, :]
            bound = jnp.sqrt(qn2 * kn2 * NORM_SLACK) + bias_max - m_min
            need = need | jnp.where(bound >= ZERO_EXP2 - BOUND_SLACK, 1, 0)
        need = jnp.where(lane < i, need, 0)
        seq_ref[0] = i
        count = jnp.int32(1)
        for j in range(n_kblocks - 2, -1, -1):
            seq_ref[count] = j
            count = count + need[0, j]
        return count - 1

    def finish(i, last, slot):
        for hh in range(2):
            consume(seq_ref[last], slot, hh)
        outs = []
        for hh in range(2):
            acc = acc_ref[hh]
            outs.append(acc[0:HEAD_DIM] * (1.0 / acc[HEAD_DIM:HEAD_DIM + 1]))
        out = jnp.concatenate(outs, axis=0).T
        o_ref[0, pl.ds(pl.multiple_of(i * tq, tq), tq), :] = out.astype(o_ref.dtype)
        fill(jnp.minimum(i + 1, n_qblocks - 1))

    def query_block(i, carry):
        last = plan(i)
        m_ref[...] = jnp.full_like(m_ref, -jnp.inf)
        acc_ref[...] = jnp.zeros_like(acc_ref)

        @pl.when(last == 0)
        def _():
            finish(i, last, slot_c)

        @pl.when(last > 0)
        def _():
            stage(seq_ref[1], slot_b, seq_ref[0], slot_c)
            n_loop = (last - 1) // 2

            def body(t, carry):
                n = 1 + 2 * t
                stage(seq_ref[n + 1], slot_a, seq_ref[n], slot_b)
                stage(seq_ref[n + 2], slot_b, seq_ref[n + 1], slot_a)
                return carry

            lax.fori_loop(0, n_loop, body, 0)
            n = 1 + 2 * n_loop

            @pl.when(last == n)
            def _():
                finish(i, last, slot_b)

            @pl.when(last == n + 1)
            def _():
                stage(seq_ref[last], slot_a, seq_ref[n], slot_b)
                finish(i, last, slot_a)

        return carry

    fill(0)
    lax.fori_loop(0, n_qblocks, query_block, 0)


def _attention(qt, k, vt, kb, kst, tq):
    B, S, _ = k.shape
    n_pairs = N_ATTN_HEADS // 2
    rows = pl.BlockSpec((1, S, LANES), lambda b, p: (b, 0, p))
    cols = pl.BlockSpec((1, LANES, S), lambda b, p: (b, p, 0))
    return pl.pallas_call(
        functools.partial(_attn_kernel, tq=tq),
        grid=(B, n_pairs),
        in_specs=[cols, rows, cols, pl.BlockSpec((1, S, LANES), lambda b, p: (b, 0, 0)),
                  pl.BlockSpec((1, 1, 8, LANES), lambda b, p: (b, p, 0, 0))],
        out_specs=rows,
        out_shape=jax.ShapeDtypeStruct((B, S, D_ATTN), BF16),
        scratch_shapes=[pltpu.VMEM((2, 2 * LANES, tq), BF16),
                        pltpu.VMEM((2, 1, tq), F32),
                        pltpu.VMEM((2, V_ROWS, tq), F32),
                        pltpu.VMEM((2, tq, tq), F32),
                        pltpu.VMEM((2, tq, tq), F32),
                        pltpu.VMEM((2, tq, tq), F32),
                        pltpu.VMEM((2, 1, tq), F32),
                        pltpu.VMEM((2, 1, tq), F32),
                        pltpu.VMEM((2, 1, tq), F32),
                        pltpu.SMEM((S // tq + 1,), jnp.int32)],
        compiler_params=pltpu.CompilerParams(
            dimension_semantics=("arbitrary", "arbitrary"), vmem_limit_bytes=VMEM_LIMIT_BYTES),
        name="fox_attention",
    )(qt, k, vt, kb, kst)


def _rglru_conv(x, cw_ref, cb_ref, xtail_ref):
    ts, width = x.shape
    n_groups = ts // 8
    x3 = x.reshape(n_groups, 8, width)
    tail = xtail_ref[...]
    sub = lax.broadcasted_iota(jnp.int32, (n_groups, 8, width), 1)
    xc3 = cb_ref[...] + cw_ref[RNN_CONV_W - 1:RNN_CONV_W, :] * x3
    for shift in range(1, RNN_CONV_W):
        kk = RNN_CONV_W - 1 - shift
        rot = pltpu.roll(x3, shift, axis=1)
        prev = jnp.concatenate([pltpu.roll(tail, shift, axis=0)[None], rot[:-1]], axis=0)
        xc3 = xc3 + cw_ref[kk:kk + 1, :] * jnp.where(sub < shift, prev, rot)
    xtail_ref[...] = x[ts - 8:ts]
    return xc3.reshape(ts, width)


def _rglru_recurrence(xc, gate, first_tile, wbd_ref, ba_ref, bx_ref, lam_ref, h_ref):
    ts, width = xc.shape
    n_groups = ts // 8
    sub = lax.broadcasted_iota(jnp.int32, (n_groups, 8, width), 1)
    xcb = xc.astype(BF16)
    r_parts, i_parts = [], []
    for g in range(width // MXU_DIM):
        z = _dot(xcb[:, g * MXU_DIM:(g + 1) * MXU_DIM], wbd_ref[g])
        r_parts.append(z[:, 0:MXU_DIM])
        i_parts.append(z[:, MXU_DIM:2 * MXU_DIM])
    r = jax.nn.sigmoid(jnp.concatenate(r_parts, axis=1) + ba_ref[...])
    ig = jax.nn.sigmoid(jnp.concatenate(i_parts, axis=1) + bx_ref[...])
    log_a = (-RG_C) * r * _softplus(-lam_ref[...])
    a = jnp.exp(log_a)
    m2 = jnp.tanh(-log_a) * (1.0 + a * a)
    mult = jnp.where(m2 > 0.0, m2 * lax.rsqrt(m2), 0.0)
    row = lax.broadcasted_iota(jnp.int32, (ts, width), 0)
    mult = jnp.where(row == jnp.where(first_tile, 0, -1), 1.0, mult)
    b = mult * ig * xc

    a3 = a.reshape(n_groups, 8, width)
    b3 = b.reshape(n_groups, 8, width)
    for d in (1, 2, 4):
        valid = sub >= d
        b3 = jnp.where(valid, a3 * pltpu.roll(b3, d, axis=1), 0.0) + b3
        a3 = jnp.where(valid, a3 * pltpu.roll(a3, d, axis=1), a3)
    h_prev = h_ref[...]
    groups = []
    for g in range(n_groups):
        h_g = a3[g] * h_prev + b3[g]
        groups.append(h_g)
        h_prev = h_g[7:8]
    h_ref[...] = h_prev
    return jnp.concatenate(groups, axis=0) * _gelu_tanh(gate)


def _outproj_kernel(ya_ref, yr_ref, x_ref, na_ref, nr_ref, wo_ref, npost_ref, nffn_ref,
                    x1_ref, h2_ref):
    tm, da = ya_ref.shape
    n_parts = 2
    part = tm // n_parts
    for r in range(n_parts):
        rows = pl.ds(r * part, part)
        ya = _rms(ya_ref[rows, :].astype(F32), na_ref[...]).astype(BF16)
        yr = _rms(yr_ref[rows, :].astype(F32), nr_ref[...]).astype(BF16)
        y = _dot(ya, wo_ref[0:da, :]) + _dot(yr, wo_ref[da:, :])
        x1 = x_ref[rows, :] + _rms(y, npost_ref[...])
        x1_ref[rows, :] = x1
        h2_ref[rows, :] = _rms(x1, nffn_ref[...]).astype(BF16)


def _out_projection(ya, yr, x, na, nr, wo, npost, nffn, tm):
    M, D = x.shape
    da, dr = ya.shape[1], yr.shape[1]
    const = lambda shape: pl.BlockSpec(shape, lambda i: (0,) * len(shape),
                                       pipeline_mode=pl.Buffered(1))
    row = lambda width: pl.BlockSpec((tm, width), lambda i: (i, 0))
    return pl.pallas_call(
        _outproj_kernel,
        grid=(M // tm,),
        in_specs=[row(da), row(dr), row(D), const((1, da)), const((1, dr)), const(wo.shape),
                  const((1, D)), const((1, D))],
        out_specs=[row(D), row(D)],
        out_shape=[jax.ShapeDtypeStruct((M, D), F32), jax.ShapeDtypeStruct((M, D), BF16)],
        compiler_params=pltpu.CompilerParams(
            dimension_semantics=("arbitrary",), vmem_limit_bytes=VMEM_LIMIT_BYTES),
        name="out_projection",
    )(ya, yr, x, na, nr, wo, npost, nffn)


def _ffn_kernel(h_ref, wg_ref, wu_ref, cw_ref, cb_ref, wd_ref, x1_ref, gain_ref,
                o_ref, acc_ref, gtail_ref, *, tiles_per_seq):
    tm = h_ref.shape[0]
    tc = wg_ref.shape[1]
    i = pl.program_id(0)
    c = pl.program_id(1)

    @pl.when(c == 0)
    def _():
        acc_ref[...] = jnp.zeros_like(acc_ref)

    h = h_ref[...]
    g = _dot(h, wg_ref[...])
    u = _dot(h, wu_ref[...])
    n_groups = tm // 8
    seq_start = (i % tiles_per_seq) == 0
    tail = jnp.where(seq_start, 0.0, gtail_ref[c])
    gtail_ref[c] = g[tm - 8:tm, :]
    g3 = g.reshape(n_groups, 8, tc)
    sub = lax.broadcasted_iota(jnp.int32, (n_groups, 8, tc), 1)
    conv = cb_ref[...] + cw_ref[FFN_CONV_W - 1:FFN_CONV_W, :] * g3
    for shift in range(1, FFN_CONV_W):
        kk = FFN_CONV_W - 1 - shift
        rot = pltpu.roll(g3, shift, axis=1)
        prev = jnp.concatenate([pltpu.roll(tail, shift, axis=0)[None], rot[:-1]], axis=0)
        conv = conv + cw_ref[kk:kk + 1, :] * jnp.where(sub < shift, prev, rot)
    act = (_gelu_tanh(conv.reshape(tm, tc)) * u).astype(BF16)
    acc_ref[...] += _dot(act, wd_ref[...])

    @pl.when(c == pl.num_programs(1) - 1)
    def _():
        o_ref[...] = x1_ref[...] + _rms(acc_ref[...], gain_ref[...])


def _ffn(h2, wg, wu, cw, cb, wd, x1, gain, tm, tc, tiles_per_seq):
    M, D = h2.shape
    dff = wg.shape[1]
    nc = dff // tc
    return pl.pallas_call(
        functools.partial(_ffn_kernel, tiles_per_seq=tiles_per_seq),
        grid=(M // tm, nc),
        in_specs=[pl.BlockSpec((tm, D), lambda i, c: (i, 0)),
                  pl.BlockSpec((D, tc), lambda i, c: (0, c)),
                  pl.BlockSpec((D, tc), lambda i, c: (0, c)),
                  pl.BlockSpec((FFN_CONV_W, tc), lambda i, c: (0, c)),
                  pl.BlockSpec((1, tc), lambda i, c: (0, c)),
                  pl.BlockSpec((tc, D), lambda i, c: (c, 0)),
                  pl.BlockSpec((tm, D), lambda i, c: (i, 0)),
                  pl.BlockSpec((1, D), lambda i, c: (0, 0))],
        out_specs=pl.BlockSpec((tm, D), lambda i, c: (i, 0)),
        out_shape=jax.ShapeDtypeStruct((M, D), F32),
        scratch_shapes=[pltpu.VMEM((tm, D), F32),
                        pltpu.VMEM((nc, 8, tc), F32)],
        compiler_params=pltpu.CompilerParams(
            dimension_semantics=("arbitrary", "arbitrary"), vmem_limit_bytes=VMEM_LIMIT_BYTES),
        name="conv_gated_mlp",
    )(h2, wg, wu, cw, cb, wd, x1, gain)


def _block_diag(w, group):
    n, r, _ = w.shape
    w = w.reshape(n // group, group, r, r)
    eye = jnp.eye(group, dtype=w.dtype)
    return jnp.einsum("gaij,ab->gaibj", w, eye).reshape(n // group, group * r, group * r)


def _layer(x, norm_mix_pre, w_in, b_forget, conv_rnn_w, conv_rnn_b, w_rg_a, b_rg_a, w_rg_x,
           b_rg_x, rg_lambda, norm_attn_out, norm_rnn_out, w_out, norm_mix_post, norm_ffn_pre,
           w_gate, w_up, conv_ffn_w, conv_ffn_b, w_down, norm_ffn_post):
    B, S, D = x.shape
    d_rnn = conv_rnn_w.shape[1]
    t_seq = min(TOKEN_TILE, S)

    nh = N_ATTN_HEADS
    w_q, w_k, w_v = w_in[:, 0:D_ATTN], w_in[:, D_ATTN:2 * D_ATTN], w_in[:, 2 * D_ATTN:3 * D_ATTN]
    w_f = w_in[:, 3 * D_ATTN:3 * D_ATTN + nh]
    w_xg = w_in[:, 3 * D_ATTN + nh:]
    w_f_rep = jnp.concatenate([w_f, w_f, w_f, jnp.zeros((D, LANES - 3 * nh), w_in.dtype)], axis=1)
    bf_rep = jnp.concatenate([b_forget, b_forget, b_forget,
                              jnp.zeros((LANES - 3 * nh,), b_forget.dtype)]).reshape(1, LANES)
    group = MXU_DIM // RNN_BLOCK
    wbd = jnp.concatenate([_block_diag(w_rg_a, group), _block_diag(w_rg_x, group)],
                          axis=2).astype(BF16)
    row = lambda v: v.reshape(1, -1)

    hsel = jnp.repeat(jnp.eye(nh, LANES, dtype=BF16), HEAD_DIM, axis=0)
    qt, k, vt, kb, y_rnn, kst = _in_projection(
        x, row(norm_mix_pre), w_q.T.astype(BF16), w_v.T.astype(BF16), w_k.astype(BF16),
        w_xg.astype(BF16), w_f_rep.astype(BF16), bf_rep, hsel,
        conv_rnn_w, row(conv_rnn_b), wbd, row(b_rg_a), row(b_rg_x), row(rg_lambda), t_seq)
    n_blocks = S // t_seq
    kst = kst[:, :, 0:2, 0:nh].reshape(B, n_blocks, 2, nh // 2, 2).transpose(0, 3, 2, 4, 1)
    kst = jnp.pad(kst.reshape(B, nh // 2, 4, n_blocks), ((0, 0), (0, 0), (0, 4), (0, LANES - n_blocks)))
    y_attn = _attention(qt, k, vt, kb, kst, t_seq)
    M = B * S
    x1, h2 = _out_projection(y_attn.reshape(M, D_ATTN), y_rnn.reshape(M, d_rnn), x.reshape(M, D),
                             row(norm_attn_out), row(norm_rnn_out), w_out.astype(BF16),
                             row(norm_mix_post), row(norm_ffn_pre), t_seq)
    out = _ffn(h2, w_gate.astype(BF16), w_up.astype(BF16), conv_ffn_w, row(conv_ffn_b),
               w_down.astype(BF16), x1, row(norm_ffn_post), t_seq, MLP_CHUNK, S // t_seq)
    return out.reshape(B, S, D)


def kernel(x, norm_mix_pre, w_in, b_forget, conv_rnn_w, conv_rnn_b, w_rg_a, b_rg_a, w_rg_x, b_rg_x, rg_lambda, norm_attn_out, norm_rnn_out, w_out, norm_mix_post, norm_ffn_pre, w_gate, w_up, conv_ffn_w, conv_ffn_b, w_down, norm_ffn_post):
    params = (norm_mix_pre, w_in, b_forget, conv_rnn_w, conv_rnn_b, w_rg_a, b_rg_a, w_rg_x, b_rg_x,
              rg_lambda, norm_attn_out, norm_rnn_out, w_out, norm_mix_post, norm_ffn_pre, w_gate,
              w_up, conv_ffn_w, conv_ffn_b, w_down, norm_ffn_post)
    for l in range(norm_mix_pre.shape[0]):
        x = _layer(x, *(p[l] for p in params))
    return x
```

```python
import functools
import math

import jax
import jax.numpy as jnp
from jax import lax
from jax.experimental import pallas as pl
from jax.experimental.pallas import tpu as pltpu

N_ATTN_HEADS = 16
HEAD_DIM = 64
D_ATTN = N_ATTN_HEADS * HEAD_DIM
N_RNN_BLOCKS = 16
RNN_BLOCK = 64
RNN_CONV_W = 4
RG_C = 8.0
FFN_CONV_W = 3
EPS = 1e-6
LOG2E = 1.4426950408889634

ZERO_EXP2 = -150.0
BOUND_SLACK = 8.0
NORM_SLACK = 1.05

LANES = 128
MXU_DIM = 256
VMEM_LIMIT_BYTES = 58 * 1024 * 1024
TOKEN_TILE = 512
MLP_CHUNK = 1024

F32 = jnp.float32
BF16 = jnp.bfloat16


def _rms(x, gain):
    return x * lax.rsqrt(jnp.mean(x * x, axis=-1, keepdims=True) + EPS) * gain


def _gelu_tanh(x):
    cdf = 0.5 * (1.0 + jnp.tanh(math.sqrt(2.0 / math.pi) * (x + 0.044715 * (x * x * x))))
    return x * cdf


def _softplus(x):
    return jnp.maximum(x, 0.0) + jnp.log1p(jnp.exp(-jnp.abs(x)))


def _split3(x):
    hi = x.astype(BF16).astype(F32)
    r = x - hi
    mid = r.astype(BF16).astype(F32)
    lo = (r - mid).astype(BF16).astype(F32)
    return hi, mid, lo


def _dot(a, b):
    return jnp.dot(a, b, preferred_element_type=F32)


def _dot_nt(a, b):
    return lax.dot_general(a, b, (((1,), (1,)), ((), ())), preferred_element_type=F32)


def _proj_kernel(x_ref, g_ref, wqt_ref, wvt_ref, wk_ref, wxg_ref, wf_ref, bf_ref, hsel_ref,
                 cw_ref, cb_ref, wbd_ref, ba_ref, bx_ref, lam_ref,
                 qt_ref, k_ref, vt_ref, kb_ref, yr_ref, kst_ref, carry_ref, xtail_ref, hstate_ref):
    tm = x_ref.shape[1]
    first_tile = pl.program_id(1) == 0

    @pl.when(first_tile)
    def _():
        carry_ref[...] = jnp.zeros_like(carry_ref)
        xtail_ref[...] = jnp.zeros_like(xtail_ref)
        hstate_ref[...] = jnp.zeros_like(hstate_ref)

    h = _rms(x_ref[0], g_ref[...]).astype(BF16)
    lane = lax.broadcasted_iota(jnp.int32, (tm, LANES), 1)

    fl = _dot(h, wf_ref[...]) + bf_ref[...]
    logf = jnp.minimum(fl, 0.0) - jnp.log1p(jnp.exp(-jnp.abs(fl)))
    logf = jnp.where(lane < 3 * N_ATTN_HEADS, logf, 0.0)
    hi, mid, lo = _split3(logf)
    pieces = jnp.concatenate([hi, mid, lo], axis=1).astype(BF16)

    xr = _dot(h, wxg_ref[:, 0:D_ATTN])
    gr = _dot(h, wxg_ref[:, D_ATTN:2 * D_ATTN])
    xc = _rglru_conv(xr, cw_ref, cb_ref, xtail_ref)
    yr_ref[0] = _rglru_recurrence(xc, gr, first_tile, wbd_ref, ba_ref, bx_ref, lam_ref,
                                  hstate_ref).astype(yr_ref.dtype)

    qt_ref[0] = (_dot_nt(wqt_ref[...], h) * (HEAD_DIM ** -0.5 * LOG2E)).astype(BF16)

    rows = lax.broadcasted_iota(jnp.int32, (tm, tm), 0)
    cols = lax.broadcasted_iota(jnp.int32, (tm, tm), 1)
    tri = jnp.where(rows >= cols, 1.0, 0.0).astype(BF16)
    cs = _dot(tri, pieces)
    c = cs[:, 0:LANES] + cs[:, LANES:2 * LANES] + cs[:, 2 * LANES:3 * LANES] + carry_ref[...]
    carry_ref[...] = c[tm - 1:tm, :]
    bias = c * (-LOG2E)
    bhi, bmid, blo = _split3(bias)
    kb = jnp.where(lane < N_ATTN_HEADS, bhi, jnp.where(lane < 2 * N_ATTN_HEADS, bmid, blo))
    kb_ref[0] = kb.astype(BF16)

    vt_ref[0] = _dot_nt(wvt_ref[...], h).astype(BF16)
    k = _dot(h, wk_ref[...]).astype(BF16)
    k_ref[0] = k
    k2 = k.astype(F32)
    kn2 = _dot((k2 * k2).astype(BF16), hsel_ref[...])
    kst_ref[0, 0] = jnp.concatenate(
        [jnp.max(kn2, axis=0, keepdims=True), jnp.max(bias, axis=0, keepdims=True),
         jnp.zeros((6, LANES), F32)], axis=0)


def _in_projection(x, gain, wqt, wvt, wk, wxg, wf, bf_rep, hsel, cw, cb, wbd, ba, bx, lam, tm):
    B, S, D = x.shape
    const = lambda a: pl.BlockSpec(a.shape, lambda b, i: (0,) * a.ndim, pipeline_mode=pl.Buffered(1))
    row = lambda width: pl.BlockSpec((1, tm, width), lambda b, i: (b, i, 0))
    col = pl.BlockSpec((1, D_ATTN, tm), lambda b, i: (b, 0, i))
    weights = (gain, wqt, wvt, wk, wxg, wf, bf_rep, hsel, cw, cb, wbd, ba, bx, lam)
    d_rnn = cw.shape[1]
    return pl.pallas_call(
        _proj_kernel,
        grid=(B, S // tm),
        in_specs=[row(D)] + [const(w) for w in weights],
        out_specs=[col, row(D_ATTN), col, row(LANES), row(d_rnn),
                   pl.BlockSpec((1, 1, 8, LANES), lambda b, i: (b, i, 0, 0))],
        out_shape=[jax.ShapeDtypeStruct((B, D_ATTN, S), BF16),
                   jax.ShapeDtypeStruct((B, S, D_ATTN), BF16),
                   jax.ShapeDtypeStruct((B, D_ATTN, S), BF16),
                   jax.ShapeDtypeStruct((B, S, LANES), BF16),
                   jax.ShapeDtypeStruct((B, S, d_rnn), BF16),
                   jax.ShapeDtypeStruct((B, S // tm, 8, LANES), F32)],
        scratch_shapes=[pltpu.VMEM((1, LANES), F32),
                        pltpu.VMEM((8, d_rnn), F32),
                        pltpu.VMEM((1, d_rnn), F32)],
        compiler_params=pltpu.CompilerParams(
            dimension_semantics=("arbitrary", "arbitrary"), vmem_limit_bytes=VMEM_LIMIT_BYTES),
        name="in_projection",
    )(x, *weights)


V_ROWS = HEAD_DIM + 16


def _attn_kernel(qt_ref, k_ref, vt_ref, kb_ref, kst_ref, o_ref,
                 rhs_ref, m_ref, acc_ref, s_a_ref, s_b_ref, s_c_ref, mb_a_ref, mb_b_ref, mb_c_ref,
                 seq_ref, *, tq):
    hp = pl.program_id(1)
    tk = tq
    n_kblocks = k_ref.shape[1] // tk
    n_qblocks = qt_ref.shape[2] // tq
    ones = jnp.ones((V_ROWS - HEAD_DIM, tk), BF16)
    slot_a = (s_a_ref, mb_a_ref)
    slot_b = (s_b_ref, mb_b_ref)
    slot_c = (s_c_ref, mb_c_ref)

    def q_block(i):
        return qt_ref[0, :, pl.ds(pl.multiple_of(i * tq, tq), tq)]

    def scores(j, slot, diagonal, hh):
        s_ref, mb_ref = slot
        start = pl.multiple_of(j * tk, tk)
        lhs = jnp.concatenate([k_ref[0, pl.ds(start, tk), :], kb_ref[0, pl.ds(start, tk), :]], axis=1)
        s = _dot(lhs, rhs_ref[hh])
        if diagonal:
            key = lax.broadcasted_iota(jnp.int32, (tk, tq), 0)
            qry = lax.broadcasted_iota(jnp.int32, (tk, tq), 1)
            s = jnp.where(key <= qry, s, -jnp.inf)
        s_ref[hh] = s
        mb_ref[hh] = jnp.max(s, axis=0, keepdims=True)

    def consume(j, slot, hh):
        s_ref, mb_ref = slot
        start = pl.multiple_of(j * tk, tk)
        m_old = m_ref[hh]
        m_new = jnp.maximum(m_old, mb_ref[hh])
        p = jnp.exp2(s_ref[hh] - m_new).astype(BF16)
        vt = jnp.concatenate(
            [vt_ref[0, hh * HEAD_DIM:(hh + 1) * HEAD_DIM, pl.ds(start, tk)], ones], axis=0)
        acc_ref[hh] = acc_ref[hh] * jnp.exp2(m_old - m_new) + _dot(vt, p)
        m_ref[hh] = m_new

    def stage(js, slot_s, jc, slot_prev):
        for hh in range(2):
            scores(js, slot_s, False, hh)
            consume(jc, slot_prev, hh)

    def fill(i):
        qt = q_block(i)
        row = lax.broadcasted_iota(jnp.int32, (LANES, tq), 0)
        for hh in range(2):
            head = 2 * hp + hh
            qh = jnp.where((row >= hh * HEAD_DIM) & (row < (hh + 1) * HEAD_DIM), qt, jnp.zeros_like(qt))
            sel = (row == head) | (row == head + N_ATTN_HEADS) | (row == head + 2 * N_ATTN_HEADS)
            rhs_ref[hh, 0:LANES, :] = qh
            rhs_ref[hh, LANES:2 * LANES, :] = jnp.where(sel, 1.0, 0.0).astype(BF16)
        for hh in range(2):
            scores(i, slot_c, True, hh)

    def plan(i):
        qt = q_block(i)
        lane = lax.broadcasted_iota(jnp.int32, (1, LANES), 1)
        need = jnp.zeros((1, LANES), jnp.int32)
        for hh in range(2):
            qf = qt[hh * HEAD_DIM:(hh + 1) * HEAD_DIM, :].astype(F32)
            qn2 = jnp.max(jnp.sum(qf * qf, axis=0, keepdims=True), axis=1, keepdims=True)
            m_min = jnp.min(mb_c_ref[hh], axis=1, keepdims=True)
            kn2 = kst_ref[0, 0, hh:hh + 1, :]
            bias_max = kst_ref[0, 0, 2 + hh:3 + hh, :]
            bound = jnp.sqrt(qn2 * kn2 * NORM_SLACK) + bias_max - m_min
            need = need | jnp.where(bound >= ZERO_EXP2 - BOUND_SLACK, 1, 0)
        need = jnp.where(lane < i, need, 0)
        seq_ref[0] = i
        count = jnp.int32(1)
        for j in range(n_kblocks - 2, -1, -1):
            seq_ref[count] = j
            count = count + need[0, j]
        return count - 1

    def finish(i, last, slot):
        for hh in range(2):
            consume(seq_ref[last], slot, hh)
        outs = []
        for hh in range(2):
            acc = acc_ref[hh]
            outs.append(acc[0:HEAD_DIM] * (1.0 / acc[HEAD_DIM:HEAD_DIM + 1]))
        out = jnp.concatenate(outs, axis=0).T
        o_ref[0, pl.ds(pl.multiple_of(i * tq, tq), tq), :] = out.astype(o_ref.dtype)
        fill(jnp.minimum(i + 1, n_qblocks - 1))

    def query_block(i, carry):
        last = plan(i)
        m_ref[...] = jnp.full_like(m_ref, -jnp.inf)
        acc_ref[...] = jnp.zeros_like(acc_ref)

        @pl.when(last == 0)
        def _():
            finish(i, last, slot_c)

        @pl.when(last > 0)
        def _():
            stage(seq_ref[1], slot_b, seq_ref[0], slot_c)
            n_loop = (last - 1) // 2

            def body(t, carry):
                n = 1 + 2 * t
                stage(seq_ref[n + 1], slot_a, seq_ref[n], slot_b)
                stage(seq_ref[n + 2], slot_b, seq_ref[n + 1], slot_a)
                return carry

            lax.fori_loop(0, n_loop, body, 0)
            n = 1 + 2 * n_loop

            @pl.when(last == n)
            def _():
                finish(i, last, slot_b)

            @pl.when(last == n + 1)
            def _():
                stage(seq_ref[last], slot_a, seq_ref[n], slot_b)
                finish(i, last, slot_a)

        return carry

    fill(0)
    lax.fori_loop(0, n_qblocks, query_block, 0)


def _attention(qt, k, vt, kb, kst, tq):
    B, S, _ = k.shape
    n_pairs = N_ATTN_HEADS // 2
    rows = pl.BlockSpec((1, S, LANES), lambda b, p: (b, 0, p))
    cols = pl.BlockSpec((1, LANES, S), lambda b, p: (b, p, 0))
    return pl.pallas_call(
        functools.partial(_attn_kernel, tq=tq),
        grid=(B, n_pairs),
        in_specs=[cols, rows, cols, pl.BlockSpec((1, S, LANES), lambda b, p: (b, 0, 0)),
                  pl.BlockSpec((1, 1, 8, LANES), lambda b, p: (b, p, 0, 0))],
        out_specs=rows,
        out_shape=jax.ShapeDtypeStruct((B, S, D_ATTN), BF16),
        scratch_shapes=[pltpu.VMEM((2, 2 * LANES, tq), BF16),
                        pltpu.VMEM((2, 1, tq), F32),
                        pltpu.VMEM((2, V_ROWS, tq), F32),
                        pltpu.VMEM((2, tq, tq), F32),
                        pltpu.VMEM((2, tq, tq), F32),
                        pltpu.VMEM((2, tq, tq), F32),
                        pltpu.VMEM((2, 1, tq), F32),
                        pltpu.VMEM((2, 1, tq), F32),
                        pltpu.VMEM((2, 1, tq), F32),
                        pltpu.SMEM((S // tq + 1,), jnp.int32)],
        compiler_params=pltpu.CompilerParams(
            dimension_semantics=("arbitrary", "arbitrary"), vmem_limit_bytes=VMEM_LIMIT_BYTES),
        name="fox_attention",
    )(qt, k, vt, kb, kst)


def _rglru_conv(x, cw_ref, cb_ref, xtail_ref):
    ts, width = x.shape
    n_groups = ts // 8
    x3 = x.reshape(n_groups, 8, width)
    tail = xtail_ref[...]
    sub = lax.broadcasted_iota(jnp.int32, (n_groups, 8, width), 1)
    xc3 = cb_ref[...] + cw_ref[RNN_CONV_W - 1:RNN_CONV_W, :] * x3
    for shift in range(1, RNN_CONV_W):
        kk = RNN_CONV_W - 1 - shift
        rot = pltpu.roll(x3, shift, axis=1)
        prev = jnp.concatenate([pltpu.roll(tail, shift, axis=0)[None], rot[:-1]], axis=0)
        xc3 = xc3 + cw_ref[kk:kk + 1, :] * jnp.where(sub < shift, prev, rot)
    xtail_ref[...] = x[ts - 8:ts]
    return xc3.reshape(ts, width)


def _rglru_recurrence(xc, gate, first_tile, wbd_ref, ba_ref, bx_ref, lam_ref, h_ref):
    ts, width = xc.shape
    n_groups = ts // 8
    sub = lax.broadcasted_iota(jnp.int32, (n_groups, 8, width), 1)
    xcb = xc.astype(BF16)
    r_parts, i_parts = [], []
    for g in range(width // MXU_DIM):
        z = _dot(xcb[:, g * MXU_DIM:(g + 1) * MXU_DIM], wbd_ref[g])
        r_parts.append(z[:, 0:MXU_DIM])
        i_parts.append(z[:, MXU_DIM:2 * MXU_DIM])
    r = jax.nn.sigmoid(jnp.concatenate(r_parts, axis=1) + ba_ref[...])
    ig = jax.nn.sigmoid(jnp.concatenate(i_parts, axis=1) + bx_ref[...])
    log_a = (-RG_C) * r * _softplus(-lam_ref[...])
    a = jnp.exp(log_a)
    m2 = jnp.tanh(-log_a) * (1.0 + a * a)
    mult = jnp.where(m2 > 0.0, m2 * lax.rsqrt(m2), 0.0)
    row = lax.broadcasted_iota(jnp.int32, (ts, width), 0)
    mult = jnp.where(row == jnp.where(first_tile, 0, -1), 1.0, mult)
    b = mult * ig * xc

    a3 = a.reshape(n_groups, 8, width)
    b3 = b.reshape(n_groups, 8, width)
    for d in (1, 2, 4):
        valid = sub >= d
        b3 = jnp.where(valid, a3 * pltpu.roll(b3, d, axis=1), 0.0) + b3
        a3 = jnp.where(valid, a3 * pltpu.roll(a3, d, axis=1), a3)
    h_prev = h_ref[...]
    groups = []
    for g in range(n_groups):
        h_g = a3[g] * h_prev + b3[g]
        groups.append(h_g)
        h_prev = h_g[7:8]
    h_ref[...] = h_prev
    return jnp.concatenate(groups, axis=0) * _gelu_tanh(gate)


def _outproj_kernel(ya_ref, yr_ref, x_ref, na_ref, nr_ref, wo_ref, npost_ref, nffn_ref,
                    x1_ref, h2_ref):
    tm, da = ya_ref.shape
    n_parts = 2
    part = tm // n_parts
    for r in range(n_parts):
        rows = pl.ds(r * part, part)
        ya = _rms(ya_ref[rows, :].astype(F32), na_ref[...]).astype(BF16)
        yr = _rms(yr_ref[rows, :].astype(F32), nr_ref[...]).astype(BF16)
        y = _dot(ya, wo_ref[0:da, :]) + _dot(yr, wo_ref[da:, :])
        x1 = x_ref[rows, :] + _rms(y, npost_ref[...])
        x1_ref[rows, :] = x1
        h2_ref[rows, :] = _rms(x1, nffn_ref[...]).astype(BF16)


def _out_projection(ya, yr, x, na, nr, wo, npost, nffn, tm):
    M, D = x.shape
    da, dr = ya.shape[1], yr.shape[1]
    const = lambda shape: pl.BlockSpec(shape, lambda i: (0,) * len(shape),
                                       pipeline_mode=pl.Buffered(1))
    row = lambda width: pl.BlockSpec((tm, width), lambda i: (i, 0))
    return pl.pallas_call(
        _outproj_kernel,
        grid=(M // tm,),
        in_specs=[row(da), row(dr), row(D), const((1, da)), const((1, dr)), const(wo.shape),
                  const((1, D)), const((1, D))],
        out_specs=[row(D), row(D)],
        out_shape=[jax.ShapeDtypeStruct((M, D), F32), jax.ShapeDtypeStruct((M, D), BF16)],
        compiler_params=pltpu.CompilerParams(
            dimension_semantics=("arbitrary",), vmem_limit_bytes=VMEM_LIMIT_BYTES),
        name="out_projection",
    )(ya, yr, x, na, nr, wo, npost, nffn)


def _ffn_kernel(h_ref, wg_ref, wu_ref, cw_ref, cb_ref, wd_ref, x1_ref, gain_ref,
                o_ref, acc_ref, gtail_ref, *, tiles_per_seq):
    tm = h_ref.shape[0]
    tc = wg_ref.shape[1]
    i = pl.program_id(0)
    c = pl.program_id(1)

    @pl.when(c == 0)
    def _():
        acc_ref[...] = jnp.zeros_like(acc_ref)

    h = h_ref[...]
    g = _dot(h, wg_ref[...])
    u = _dot(h, wu_ref[...])
    n_groups = tm // 8
    seq_start = (i % tiles_per_seq) == 0
    tail = jnp.where(seq_start, 0.0, gtail_ref[c])
    gtail_ref[c] = g[tm - 8:tm, :]
    g3 = g.reshape(n_groups, 8, tc)
    sub = lax.broadcasted_iota(jnp.int32, (n_groups, 8, tc), 1)
    conv = cb_ref[...] + cw_ref[FFN_CONV_W - 1:FFN_CONV_W, :] * g3
    for shift in range(1, FFN_CONV_W):
        kk = FFN_CONV_W - 1 - shift
        rot = pltpu.roll(g3, shift, axis=1)
        prev = jnp.concatenate([pltpu.roll(tail, shift, axis=0)[None], rot[:-1]], axis=0)
        conv = conv + cw_ref[kk:kk + 1, :] * jnp.where(sub < shift, prev, rot)
    act = (_gelu_tanh(conv.reshape(tm, tc)) * u).astype(BF16)
    acc_ref[...] += _dot(act, wd_ref[...])

    @pl.when(c == pl.num_programs(1) - 1)
    def _():
        o_ref[...] = x1_ref[...] + _rms(acc_ref[...], gain_ref[...])


def _ffn(h2, wg, wu, cw, cb, wd, x1, gain, tm, tc, tiles_per_seq):
    M, D = h2.shape
    dff = wg.shape[1]
    nc = dff // tc
    return pl.pallas_call(
        functools.partial(_ffn_kernel, tiles_per_seq=tiles_per_seq),
        grid=(M // tm, nc),
        in_specs=[pl.BlockSpec((tm, D), lambda i, c: (i, 0)),
                  pl.BlockSpec((D, tc), lambda i, c: (0, c)),
                  pl.BlockSpec((D, tc), lambda i, c: (0, c)),
                  pl.BlockSpec((FFN_CONV_W, tc), lambda i, c: (0, c)),
                  pl.BlockSpec((1, tc), lambda i, c: (0, c)),
                  pl.BlockSpec((tc, D), lambda i, c: (c, 0)),
                  pl.BlockSpec((tm, D), lambda i, c: (i, 0)),
                  pl.BlockSpec((1, D), lambda i, c: (0, 0))],
        out_specs=pl.BlockSpec((tm, D), lambda i, c: (i, 0)),
        out_shape=jax.ShapeDtypeStruct((M, D), F32),
        scratch_shapes=[pltpu.VMEM((tm, D), F32),
                        pltpu.VMEM((nc, 8, tc), F32)],
        compiler_params=pltpu.CompilerParams(
            dimension_semantics=("arbitrary", "arbitrary"), vmem_limit_bytes=VMEM_LIMIT_BYTES),
        name="conv_gated_mlp",
    )(h2, wg, wu, cw, cb, wd, x1, gain)


def _block_diag(w, group):
    n, r, _ = w.shape
    w = w.reshape(n // group, group, r, r)
    eye = jnp.eye(group, dtype=w.dtype)
    return jnp.einsum("gaij,ab->gaibj", w, eye).reshape(n // group, group * r, group * r)


def _layer(x, norm_mix_pre, w_in, b_forget, conv_rnn_w, conv_rnn_b, w_rg_a, b_rg_a, w_rg_x,
           b_rg_x, rg_lambda, norm_attn_out, norm_rnn_out, w_out, norm_mix_post, norm_ffn_pre,
           w_gate, w_up, conv_ffn_w, conv_ffn_b, w_down, norm_ffn_post):
    B, S, D = x.shape
    d_rnn = conv_rnn_w.shape[1]
    t_seq = min(TOKEN_TILE, S)

    nh = N_ATTN_HEADS
    w_q, w_k, w_v = w_in[:, 0:D_ATTN], w_in[:, D_ATTN:2 * D_ATTN], w_in[:, 2 * D_ATTN:3 * D_ATTN]
    w_f = w_in[:, 3 * D_ATTN:3 * D_ATTN + nh]
    w_xg = w_in[:, 3 * D_ATTN + nh:]
    w_f_rep = jnp.concatenate([w_f, w_f, w_f, jnp.zeros((D, LANES - 3 * nh), w_in.dtype)], axis=1)
    bf_rep = jnp.concatenate([b_forget, b_forget, b_forget,
                              jnp.zeros((LANES - 3 * nh,), b_forget.dtype)]).reshape(1, LANES)
    group = MXU_DIM // RNN_BLOCK
    wbd = jnp.concatenate([_block_diag(w_rg_a, group), _block_diag(w_rg_x, group)],
                          axis=2).astype(BF16)
    row = lambda v: v.reshape(1, -1)

    hsel = jnp.repeat(jnp.eye(nh, LANES, dtype=BF16), HEAD_DIM, axis=0)
    qt, k, vt, kb, y_rnn, kst = _in_projection(
        x, row(norm_mix_pre), w_q.astype(BF16).T, w_v.astype(BF16).T, w_k.astype(BF16),
        w_xg.astype(BF16), w_f_rep.astype(BF16), bf_rep, hsel,
        conv_rnn_w, row(conv_rnn_b), wbd, row(b_rg_a), row(b_rg_x), row(rg_lambda), t_seq)
    n_blocks = S // t_seq
    kst = kst[:, :, 0:2, 0:nh].reshape(B, n_blocks, 2, nh // 2, 2).transpose(0, 3, 2, 4, 1)
    kst = jnp.pad(kst.reshape(B, nh // 2, 4, n_blocks), ((0, 0), (0, 0), (0, 4), (0, LANES - n_blocks)))
    y_attn = _attention(qt, k, vt, kb, kst, t_seq)
    M = B * S
    x1, h2 = _out_projection(y_attn.reshape(M, D_ATTN), y_rnn.reshape(M, d_rnn), x.reshape(M, D),
                             row(norm_attn_out), row(norm_rnn_out), w_out.astype(BF16),
                             row(norm_mix_post), row(norm_ffn_pre), t_seq)
    out = _ffn(h2, w_gate.astype(BF16), w_up.astype(BF16), conv_ffn_w, row(conv_ffn_b),
               w_down.astype(BF16), x1, row(norm_ffn_post), t_seq, MLP_CHUNK, S // t_seq)
    return out.reshape(B, S, D)


def kernel(x, norm_mix_pre, w_in, b_forget, conv_rnn_w, conv_rnn_b, w_rg_a, b_rg_a, w_rg_x, b_rg_x, rg_lambda, norm_attn_out, norm_rnn_out, w_out, norm_mix_post, norm_ffn_pre, w_gate, w_up, conv_ffn_w, conv_ffn_b, w_down, norm_ffn_post):
    params = (norm_mix_pre, w_in, b_forget, conv_rnn_w, conv_rnn_b, w_rg_a, b_rg_a, w_rg_x, b_rg_x,
              rg_lambda, norm_attn_out, norm_rnn_out, w_out, norm_mix_post, norm_ffn_pre, w_gate,
              w_up, conv_ffn_w, conv_ffn_b, w_down, norm_ffn_post)
    for l in range(norm_mix_pre.shape[0]):
        x = _layer(x, *(p[l] for p in params))
    return x
```

```python
import functools
import math

import jax
import jax.numpy as jnp
from jax import lax
from jax.experimental import pallas as pl
from jax.experimental.pallas import tpu as pltpu

N_ATTN_HEADS = 16
HEAD_DIM = 64
D_ATTN = N_ATTN_HEADS * HEAD_DIM
N_RNN_BLOCKS = 16
RNN_BLOCK = 64
RNN_CONV_W = 4
RG_C = 8.0
FFN_CONV_W = 3
EPS = 1e-6
LOG2E = 1.4426950408889634

ZERO_EXP2 = -150.0
BOUND_SLACK = 8.0
NORM_SLACK = 1.05

LANES = 128
MXU_DIM = 256
VMEM_LIMIT_BYTES = 58 * 1024 * 1024
TOKEN_TILE = 512
MLP_CHUNK = 1024

F32 = jnp.float32
BF16 = jnp.bfloat16


def _rms(x, gain):
    return x * lax.rsqrt(jnp.mean(x * x, axis=-1, keepdims=True) + EPS) * gain


def _gelu_tanh(x):
    cdf = 0.5 * (1.0 + jnp.tanh(math.sqrt(2.0 / math.pi) * (x + 0.044715 * (x * x * x))))
    return x * cdf


def _softplus(x):
    return jnp.maximum(x, 0.0) + jnp.log1p(jnp.exp(-jnp.abs(x)))


def _split3(x):
    hi = x.astype(BF16).astype(F32)
    r = x - hi
    mid = r.astype(BF16).astype(F32)
    lo = (r - mid).astype(BF16).astype(F32)
    return hi, mid, lo


def _dot(a, b):
    return jnp.dot(a, b, preferred_element_type=F32)


def _dot_nt(a, b):
    return lax.dot_general(a, b, (((1,), (1,)), ((), ())), preferred_element_type=F32)


def _proj_kernel(x_ref, g_ref, wqt_ref, wvt_ref, wk_ref, wxg_ref, wf_ref, bf_ref, hsel_ref,
                 cw_ref, cb_ref, wbd_ref, ba_ref, bx_ref, lam_ref,
                 qt_ref, k_ref, vt_ref, kb_ref, yr_ref, kst_ref, carry_ref, xtail_ref, hstate_ref):
    tm = x_ref.shape[1]
    first_tile = pl.program_id(1) == 0

    @pl.when(first_tile)
    def _():
        carry_ref[...] = jnp.zeros_like(carry_ref)
        xtail_ref[...] = jnp.zeros_like(xtail_ref)
        hstate_ref[...] = jnp.zeros_like(hstate_ref)

    h = _rms(x_ref[0], g_ref[...]).astype(BF16)
    lane = lax.broadcasted_iota(jnp.int32, (tm, LANES), 1)

    fl = _dot(h, wf_ref[...]) + bf_ref[...]
    logf = jnp.minimum(fl, 0.0) - jnp.log1p(jnp.exp(-jnp.abs(fl)))
    logf = jnp.where(lane < 3 * N_ATTN_HEADS, logf, 0.0)
    hi, mid, lo = _split3(logf)
    pieces = jnp.concatenate([hi, mid, lo], axis=1).astype(BF16)

    xr = _dot(h, wxg_ref[:, 0:D_ATTN])
    gr = _dot(h, wxg_ref[:, D_ATTN:2 * D_ATTN])
    xc = _rglru_conv(xr, cw_ref, cb_ref, xtail_ref)
    yr_ref[0] = _rglru_recurrence(xc, gr, first_tile, wbd_ref, ba_ref, bx_ref, lam_ref,
                                  hstate_ref).astype(yr_ref.dtype)

    qt_ref[0] = (_dot_nt(wqt_ref[0], h) * (HEAD_DIM ** -0.5 * LOG2E)).astype(BF16)

    rows = lax.broadcasted_iota(jnp.int32, (tm, tm), 0)
    cols = lax.broadcasted_iota(jnp.int32, (tm, tm), 1)
    tri = jnp.where(rows >= cols, 1.0, 0.0).astype(BF16)
    cs = _dot(tri, pieces)
    c = cs[:, 0:LANES] + cs[:, LANES:2 * LANES] + cs[:, 2 * LANES:3 * LANES] + carry_ref[...]
    carry_ref[...] = c[tm - 1:tm, :]
    bias = c * (-LOG2E)
    bhi, bmid, blo = _split3(bias)
    kb = jnp.where(lane < N_ATTN_HEADS, bhi, jnp.where(lane < 2 * N_ATTN_HEADS, bmid, blo))
    kb_ref[0] = kb.astype(BF16)

    vt_ref[0] = _dot_nt(wvt_ref[0], h).astype(BF16)
    k = _dot(h, wk_ref[0]).astype(BF16)
    k_ref[0] = k
    k2 = k.astype(F32)
    kn2 = _dot((k2 * k2).astype(BF16), hsel_ref[...])
    kst_ref[0, 0] = jnp.concatenate(
        [jnp.max(kn2, axis=0, keepdims=True), jnp.max(bias, axis=0, keepdims=True),
         jnp.zeros((6, LANES), F32)], axis=0)


def _qkv_weight_kernel(w_ref, wt_ref, wc_ref):
    w = w_ref[...]
    wc_ref[0] = w.astype(BF16)
    wt_ref[0] = w.T.astype(BF16)


def _qkv_weights(w_in, rows=512):
    D = w_in.shape[0]
    return pl.pallas_call(
        _qkv_weight_kernel,
        grid=(3, D // rows),
        in_specs=[pl.BlockSpec((rows, D_ATTN), lambda j, r: (r, j))],
        out_specs=[pl.BlockSpec((1, D_ATTN, rows), lambda j, r: (j, 0, r)),
                   pl.BlockSpec((1, rows, D_ATTN), lambda j, r: (j, r, 0))],
        out_shape=[jax.ShapeDtypeStruct((3, D_ATTN, D), BF16),
                   jax.ShapeDtypeStruct((3, D, D_ATTN), BF16)],
        compiler_params=pltpu.CompilerParams(dimension_semantics=("arbitrary", "arbitrary")),
        name="qkv_weight_prep",
    )(w_in)


def _in_projection(x, gain, w_t, w_c, wxg, wf, bf_rep, hsel, cw, cb, wbd, ba, bx, lam, tm):
    B, S, D = x.shape
    const = lambda a: pl.BlockSpec(a.shape, lambda b, i: (0,) * a.ndim, pipeline_mode=pl.Buffered(1))
    plane = lambda a, n: pl.BlockSpec((1,) + a.shape[1:], lambda b, i: (n, 0, 0),
                                      pipeline_mode=pl.Buffered(1))
    row = lambda width: pl.BlockSpec((1, tm, width), lambda b, i: (b, i, 0))
    col = pl.BlockSpec((1, D_ATTN, tm), lambda b, i: (b, 0, i))
    small = (wxg, wf, bf_rep, hsel, cw, cb, wbd, ba, bx, lam)
    weights = (gain, w_t, w_t, w_c) + small
    d_rnn = cw.shape[1]
    return pl.pallas_call(
        _proj_kernel,
        grid=(B, S // tm),
        in_specs=[row(D), const(gain), plane(w_t, 0), plane(w_t, 2), plane(w_c, 1)]
                 + [const(w) for w in small],
        out_specs=[col, row(D_ATTN), col, row(LANES), row(d_rnn),
                   pl.BlockSpec((1, 1, 8, LANES), lambda b, i: (b, i, 0, 0))],
        out_shape=[jax.ShapeDtypeStruct((B, D_ATTN, S), BF16),
                   jax.ShapeDtypeStruct((B, S, D_ATTN), BF16),
                   jax.ShapeDtypeStruct((B, D_ATTN, S), BF16),
                   jax.ShapeDtypeStruct((B, S, LANES), BF16),
                   jax.ShapeDtypeStruct((B, S, d_rnn), BF16),
                   jax.ShapeDtypeStruct((B, S // tm, 8, LANES), F32)],
        scratch_shapes=[pltpu.VMEM((1, LANES), F32),
                        pltpu.VMEM((8, d_rnn), F32),
                        pltpu.VMEM((1, d_rnn), F32)],
        compiler_params=pltpu.CompilerParams(
            dimension_semantics=("arbitrary", "arbitrary"), vmem_limit_bytes=VMEM_LIMIT_BYTES),
        name="in_projection",
    )(x, *weights)


V_ROWS = HEAD_DIM + 16


def _attn_kernel(qt_ref, k_ref, vt_ref, kb_ref, kst_ref, o_ref,
                 rhs_ref, m_ref, acc_ref, s_a_ref, s_b_ref, s_c_ref, mb_a_ref, mb_b_ref, mb_c_ref,
                 seq_ref, *, tq):
    hp = pl.program_id(1)
    tk = tq
    n_kblocks = k_ref.shape[1] // tk
    n_qblocks = qt_ref.shape[2] // tq
    ones = jnp.ones((V_ROWS - HEAD_DIM, tk), BF16)
    slot_a = (s_a_ref, mb_a_ref)
    slot_b = (s_b_ref, mb_b_ref)
    slot_c = (s_c_ref, mb_c_ref)

    def q_block(i):
        return qt_ref[0, :, pl.ds(pl.multiple_of(i * tq, tq), tq)]

    def scores(j, slot, diagonal, hh):
        s_ref, mb_ref = slot
        start = pl.multiple_of(j * tk, tk)
        lhs = jnp.concatenate([k_ref[0, pl.ds(start, tk), :], kb_ref[0, pl.ds(start, tk), :]], axis=1)
        s = _dot(lhs, rhs_ref[hh])
        if diagonal:
            key = lax.broadcasted_iota(jnp.int32, (tk, tq), 0)
            qry = lax.broadcasted_iota(jnp.int32, (tk, tq), 1)
            s = jnp.where(key <= qry, s, -jnp.inf)
        s_ref[hh] = s
        mb_ref[hh] = jnp.max(s, axis=0, keepdims=True)

    def consume(j, slot, hh):
        s_ref, mb_ref = slot
        start = pl.multiple_of(j * tk, tk)
        m_old = m_ref[hh]
        m_new = jnp.maximum(m_old, mb_ref[hh])
        p = jnp.exp2(s_ref[hh] - m_new).astype(BF16)
        vt = jnp.concatenate(
            [vt_ref[0, hh * HEAD_DIM:(hh + 1) * HEAD_DIM, pl.ds(start, tk)], ones], axis=0)
        acc_ref[hh] = acc_ref[hh] * jnp.exp2(m_old - m_new) + _dot(vt, p)
        m_ref[hh] = m_new

    def stage(js, slot_s, jc, slot_prev):
        for hh in range(2):
            scores(js, slot_s, False, hh)
            consume(jc, slot_prev, hh)

    def fill(i):
        qt = q_block(i)
        row = lax.broadcasted_iota(jnp.int32, (LANES, tq), 0)
        for hh in range(2):
            head = 2 * hp + hh
            qh = jnp.where((row >= hh * HEAD_DIM) & (row < (hh + 1) * HEAD_DIM), qt, jnp.zeros_like(qt))
            sel = (row == head) | (row == head + N_ATTN_HEADS) | (row == head + 2 * N_ATTN_HEADS)
            rhs_ref[hh, 0:LANES, :] = qh
            rhs_ref[hh, LANES:2 * LANES, :] = jnp.where(sel, 1.0, 0.0).astype(BF16)
        for hh in range(2):
            scores(i, slot_c, True, hh)

    def plan(i):
        qt = q_block(i)
        lane = lax.broadcasted_iota(jnp.int32, (1, LANES), 1)
        need = jnp.zeros((1, LANES), jnp.int32)
        for hh in range(2):
            qf = qt[hh * HEAD_DIM:(hh + 1) * HEAD_DIM, :].astype(F32)
            qn2 = jnp.max(jnp.sum(qf * qf, axis=0, keepdims=True), axis=1, keepdims=True)
            m_min = jnp.min(mb_c_ref[hh], axis=1, keepdims=True)
            kn2 = kst_ref[0, 0, hh:hh + 1, :]
            bias_max = kst_ref[0, 0, 2 + hh:3 + hh, :]
            bound = jnp.sqrt(qn2 * kn2 * NORM_SLACK) + bias_max - m_min
            need = need | jnp.where(bound >= ZERO_EXP2 - BOUND_SLACK, 1, 0)
        need = jnp.where(lane < i, need, 0)
        seq_ref[0] = i
        count = jnp.int32(1)
        for j in range(n_kblocks - 2, -1, -1):
            seq_ref[count] = j
            count = count + need[0, j]
        return count - 1

    def finish(i, last, slot):
        for hh in range(2):
            consume(seq_ref[last], slot, hh)
        outs = []
        for hh in range(2):
            acc = acc_ref[hh]
            outs.append(acc[0:HEAD_DIM] * (1.0 / acc[HEAD_DIM:HEAD_DIM + 1]))
        out = jnp.concatenate(outs, axis=0).T
        o_ref[0, pl.ds(pl.multiple_of(i * tq, tq), tq), :] = out.astype(o_ref.dtype)
        fill(jnp.minimum(i + 1, n_qblocks - 1))

    def query_block(i, carry):
        last = plan(i)
        m_ref[...] = jnp.full_like(m_ref, -jnp.inf)
        acc_ref[...] = jnp.zeros_like(acc_ref)

        @pl.when(last == 0)
        def _():
            finish(i, last, slot_c)

        @pl.when(last > 0)
        def _():
            stage(seq_ref[1], slot_b, seq_ref[0], slot_c)
            n_loop = (last - 1) // 2

            def body(t, carry):
                n = 1 + 2 * t
                stage(seq_ref[n + 1], slot_a, seq_ref[n], slot_b)
                stage(seq_ref[n + 2], slot_b, seq_ref[n + 1], slot_a)
                return carry

            lax.fori_loop(0, n_loop, body, 0)
            n = 1 + 2 * n_loop

            @pl.when(last == n)
            def _():
                finish(i, last, slot_b)

            @pl.when(last == n + 1)
            def _():
                stage(seq_ref[last], slot_a, seq_ref[n], slot_b)
                finish(i, last, slot_a)

        return carry

    fill(0)
    lax.fori_loop(0, n_qblocks, query_block, 0)


def _attention(qt, k, vt, kb, kst, tq):
    B, S, _ = k.shape
    n_pairs = N_ATTN_HEADS // 2
    rows = pl.BlockSpec((1, S, LANES), lambda b, p: (b, 0, p))
    cols = pl.BlockSpec((1, LANES, S), lambda b, p: (b, p, 0))
    return pl.pallas_call(
        functools.partial(_attn_kernel, tq=tq),
        grid=(B, n_pairs),
        in_specs=[cols, rows, cols, pl.BlockSpec((1, S, LANES), lambda b, p: (b, 0, 0)),
                  pl.BlockSpec((1, 1, 8, LANES), lambda b, p: (b, p, 0, 0))],
        out_specs=rows,
        out_shape=jax.ShapeDtypeStruct((B, S, D_ATTN), BF16),
        scratch_shapes=[pltpu.VMEM((2, 2 * LANES, tq), BF16),
                        pltpu.VMEM((2, 1, tq), F32),
                        pltpu.VMEM((2, V_ROWS, tq), F32),
                        pltpu.VMEM((2, tq, tq), F32),
                        pltpu.VMEM((2, tq, tq), F32),
                        pltpu.VMEM((2, tq, tq), F32),
                        pltpu.VMEM((2, 1, tq), F32),
                        pltpu.VMEM((2, 1, tq), F32),
                        pltpu.VMEM((2, 1, tq), F32),
                        pltpu.SMEM((S // tq + 1,), jnp.int32)],
        compiler_params=pltpu.CompilerParams(
            dimension_semantics=("arbitrary", "arbitrary"), vmem_limit_bytes=VMEM_LIMIT_BYTES),
        name="fox_attention",
    )(qt, k, vt, kb, kst)


def _rglru_conv(x, cw_ref, cb_ref, xtail_ref):
    ts, width = x.shape
    n_groups = ts // 8
    x3 = x.reshape(n_groups, 8, width)
    tail = xtail_ref[...]
    sub = lax.broadcasted_iota(jnp.int32, (n_groups, 8, width), 1)
    xc3 = cb_ref[...] + cw_ref[RNN_CONV_W - 1:RNN_CONV_W, :] * x3
    for shift in range(1, RNN_CONV_W):
        kk = RNN_CONV_W - 1 - shift
        rot = pltpu.roll(x3, shift, axis=1)
        prev = jnp.concatenate([pltpu.roll(tail, shift, axis=0)[None], rot[:-1]], axis=0)
        xc3 = xc3 + cw_ref[kk:kk + 1, :] * jnp.where(sub < shift, prev, rot)
    xtail_ref[...] = x[ts - 8:ts]
    return xc3.reshape(ts, width)


def _rglru_recurrence(xc, gate, first_tile, wbd_ref, ba_ref, bx_ref, lam_ref, h_ref):
    ts, width = xc.shape
    n_groups = ts // 8
    sub = lax.broadcasted_iota(jnp.int32, (n_groups, 8, width), 1)
    xcb = xc.astype(BF16)
    r_parts, i_parts = [], []
    for g in range(width // MXU_DIM):
        z = _dot(xcb[:, g * MXU_DIM:(g + 1) * MXU_DIM], wbd_ref[g])
        r_parts.append(z[:, 0:MXU_DIM])
        i_parts.append(z[:, MXU_DIM:2 * MXU_DIM])
    r = jax.nn.sigmoid(jnp.concatenate(r_parts, axis=1) + ba_ref[...])
    ig = jax.nn.sigmoid(jnp.concatenate(i_parts, axis=1) + bx_ref[...])
    log_a = (-RG_C) * r * _softplus(-lam_ref[...])
    a = jnp.exp(log_a)
    m2 = jnp.tanh(-log_a) * (1.0 + a * a)
    mult = jnp.where(m2 > 0.0, m2 * lax.rsqrt(m2), 0.0)
    row = lax.broadcasted_iota(jnp.int32, (ts, width), 0)
    mult = jnp.where(row == jnp.where(first_tile, 0, -1), 1.0, mult)
    b = mult * ig * xc

    a3 = a.reshape(n_groups, 8, width)
    b3 = b.reshape(n_groups, 8, width)
    for d in (1, 2, 4):
        valid = sub >= d
        b3 = jnp.where(valid, a3 * pltpu.roll(b3, d, axis=1), 0.0) + b3
        a3 = jnp.where(valid, a3 * pltpu.roll(a3, d, axis=1), a3)
    h_prev = h_ref[...]
    groups = []
    for g in range(n_groups):
        h_g = a3[g] * h_prev + b3[g]
        groups.append(h_g)
        h_prev = h_g[7:8]
    h_ref[...] = h_prev
    return jnp.concatenate(groups, axis=0) * _gelu_tanh(gate)


def _outproj_kernel(ya_ref, yr_ref, x_ref, na_ref, nr_ref, wo_ref, npost_ref, nffn_ref,
                    x1_ref, h2_ref):
    tm, da = ya_ref.shape
    n_parts = 2
    part = tm // n_parts
    for r in range(n_parts):
        rows = pl.ds(r * part, part)
        ya = _rms(ya_ref[rows, :].astype(F32), na_ref[...]).astype(BF16)
        yr = _rms(yr_ref[rows, :].astype(F32), nr_ref[...]).astype(BF16)
        y = _dot(ya, wo_ref[0:da, :]) + _dot(yr, wo_ref[da:, :])
        x1 = x_ref[rows, :] + _rms(y, npost_ref[...])
        x1_ref[rows, :] = x1
        h2_ref[rows, :] = _rms(x1, nffn_ref[...]).astype(BF16)


def _out_projection(ya, yr, x, na, nr, wo, npost, nffn, tm):
    M, D = x.shape
    da, dr = ya.shape[1], yr.shape[1]
    const = lambda shape: pl.BlockSpec(shape, lambda i: (0,) * len(shape),
                                       pipeline_mode=pl.Buffered(1))
    row = lambda width: pl.BlockSpec((tm, width), lambda i: (i, 0))
    return pl.pallas_call(
        _outproj_kernel,
        grid=(M // tm,),
        in_specs=[row(da), row(dr), row(D), const((1, da)), const((1, dr)), const(wo.shape),
                  const((1, D)), const((1, D))],
        out_specs=[row(D), row(D)],
        out_shape=[jax.ShapeDtypeStruct((M, D), F32), jax.ShapeDtypeStruct((M, D), BF16)],
        compiler_params=pltpu.CompilerParams(
            dimension_semantics=("arbitrary",), vmem_limit_bytes=VMEM_LIMIT_BYTES),
        name="out_projection",
    )(ya, yr, x, na, nr, wo, npost, nffn)


def _ffn_kernel(h_ref, wg_ref, wu_ref, cw_ref, cb_ref, wd_ref, x1_ref, gain_ref,
                o_ref, acc_ref, gtail_ref, *, tiles_per_seq):
    tm = h_ref.shape[0]
    tc = wg_ref.shape[1]
    i = pl.program_id(0)
    c = pl.program_id(1)

    @pl.when(c == 0)
    def _():
        acc_ref[...] = jnp.zeros_like(acc_ref)

    h = h_ref[...]
    g = _dot(h, wg_ref[...])
    u = _dot(h, wu_ref[...])
    n_groups = tm // 8
    seq_start = (i % tiles_per_seq) == 0
    tail = jnp.where(seq_start, 0.0, gtail_ref[c])
    gtail_ref[c] = g[tm - 8:tm, :]
    g3 = g.reshape(n_groups, 8, tc)
    sub = lax.broadcasted_iota(jnp.int32, (n_groups, 8, tc), 1)
    conv = cb_ref[...] + cw_ref[FFN_CONV_W - 1:FFN_CONV_W, :] * g3
    for shift in range(1, FFN_CONV_W):
        kk = FFN_CONV_W - 1 - shift
        rot = pltpu.roll(g3, shift, axis=1)
        prev = jnp.concatenate([pltpu.roll(tail, shift, axis=0)[None], rot[:-1]], axis=0)
        conv = conv + cw_ref[kk:kk + 1, :] * jnp.where(sub < shift, prev, rot)
    act = (_gelu_tanh(conv.reshape(tm, tc)) * u).astype(BF16)
    acc_ref[...] += _dot(act, wd_ref[...])

    @pl.when(c == pl.num_programs(1) - 1)
    def _():
        o_ref[...] = x1_ref[...] + _rms(acc_ref[...], gain_ref[...])


def _ffn(h2, wg, wu, cw, cb, wd, x1, gain, tm, tc, tiles_per_seq):
    M, D = h2.shape
    dff = wg.shape[1]
    nc = dff // tc
    return pl.pallas_call(
        functools.partial(_ffn_kernel, tiles_per_seq=tiles_per_seq),
        grid=(M // tm, nc),
        in_specs=[pl.BlockSpec((tm, D), lambda i, c: (i, 0)),
                  pl.BlockSpec((D, tc), lambda i, c: (0, c)),
                  pl.BlockSpec((D, tc), lambda i, c: (0, c)),
                  pl.BlockSpec((FFN_CONV_W, tc), lambda i, c: (0, c)),
                  pl.BlockSpec((1, tc), lambda i, c: (0, c)),
                  pl.BlockSpec((tc, D), lambda i, c: (c, 0)),
                  pl.BlockSpec((tm, D), lambda i, c: (i, 0)),
                  pl.BlockSpec((1, D), lambda i, c: (0, 0))],
        out_specs=pl.BlockSpec((tm, D), lambda i, c: (i, 0)),
        out_shape=jax.ShapeDtypeStruct((M, D), F32),
        scratch_shapes=[pltpu.VMEM((tm, D), F32),
                        pltpu.VMEM((nc, 8, tc), F32)],
        compiler_params=pltpu.CompilerParams(
            dimension_semantics=("arbitrary", "arbitrary"), vmem_limit_bytes=VMEM_LIMIT_BYTES),
        name="conv_gated_mlp",
    )(h2, wg, wu, cw, cb, wd, x1, gain)


def _block_diag(w, group):
    n, r, _ = w.shape
    w = w.reshape(n // group, group, r, r)
    eye = jnp.eye(group, dtype=w.dtype)
    return jnp.einsum("gaij,ab->gaibj", w, eye).reshape(n // group, group * r, group * r)


def _layer(x, norm_mix_pre, w_in, b_forget, conv_rnn_w, conv_rnn_b, w_rg_a, b_rg_a, w_rg_x,
           b_rg_x, rg_lambda, norm_attn_out, norm_rnn_out, w_out, norm_mix_post, norm_ffn_pre,
           w_gate, w_up, conv_ffn_w, conv_ffn_b, w_down, norm_ffn_post):
    B, S, D = x.shape
    d_rnn = conv_rnn_w.shape[1]
    t_seq = min(TOKEN_TILE, S)

    nh = N_ATTN_HEADS
    w_f = w_in[:, 3 * D_ATTN:3 * D_ATTN + nh]
    w_xg = w_in[:, 3 * D_ATTN + nh:]
    w_f_rep = jnp.concatenate([w_f, w_f, w_f, jnp.zeros((D, LANES - 3 * nh), w_in.dtype)], axis=1)
    bf_rep = jnp.concatenate([b_forget, b_forget, b_forget,
                              jnp.zeros((LANES - 3 * nh,), b_forget.dtype)]).reshape(1, LANES)
    group = MXU_DIM // RNN_BLOCK
    wbd = jnp.concatenate([_block_diag(w_rg_a, group), _block_diag(w_rg_x, group)],
                          axis=2).astype(BF16)
    row = lambda v: v.reshape(1, -1)

    hsel = jnp.repeat(jnp.eye(nh, LANES, dtype=BF16), HEAD_DIM, axis=0)
    w_t, w_c = _qkv_weights(w_in)
    qt, k, vt, kb, y_rnn, kst = _in_projection(
        x, row(norm_mix_pre), w_t, w_c, w_xg.astype(BF16), w_f_rep.astype(BF16), bf_rep, hsel,
        conv_rnn_w, row(conv_rnn_b), wbd, row(b_rg_a), row(b_rg_x), row(rg_lambda), t_seq)
    n_blocks = S // t_seq
    kst = kst[:, :, 0:2, 0:nh].reshape(B, n_blocks, 2, nh // 2, 2).transpose(0, 3, 2, 4, 1)
    kst = jnp.pad(kst.reshape(B, nh // 2, 4, n_blocks), ((0, 0), (0, 0), (0, 4), (0, LANES - n_blocks)))
    y_attn = _attention(qt, k, vt, kb, kst, t_seq)
    M = B * S
    x1, h2 = _out_projection(y_attn.reshape(M, D_ATTN), y_rnn.reshape(M, d_rnn), x.reshape(M, D),
                             row(norm_attn_out), row(norm_rnn_out), w_out.astype(BF16),
                             row(norm_mix_post), row(norm_ffn_pre), t_seq)
    out = _ffn(h2, w_gate.astype(BF16), w_up.astype(BF16), conv_ffn_w, row(conv_ffn_b),
               w_down.astype(BF16), x1, row(norm_ffn_post), t_seq, MLP_CHUNK, S // t_seq)
    return out.reshape(B, S, D)


def kernel(x, norm_mix_pre, w_in, b_forget, conv_rnn_w, conv_rnn_b, w_rg_a, b_rg_a, w_rg_x, b_rg_x, rg_lambda, norm_attn_out, norm_rnn_out, w_out, norm_mix_post, norm_ffn_pre, w_gate, w_up, conv_ffn_w, conv_ffn_b, w_down, norm_ffn_post):
    params = (norm_mix_pre, w_in, b_forget, conv_rnn_w, conv_rnn_b, w_rg_a, b_rg_a, w_rg_x, b_rg_x,
              rg_lambda, norm_attn_out, norm_rnn_out, w_out, norm_mix_post, norm_ffn_pre, w_gate,
              w_up, conv_ffn_w, conv_ffn_b, w_down, norm_ffn_post)
    for l in range(norm_mix_pre.shape[0]):
        x = _layer(x, *(p[l] for p in params))
    return x
```

```python
import functools
import math

import jax
import jax.numpy as jnp
from jax import lax
from jax.experimental import pallas as pl
from jax.experimental.pallas import tpu as pltpu

N_ATTN_HEADS = 16
HEAD_DIM = 64
D_ATTN = N_ATTN_HEADS * HEAD_DIM
N_RNN_BLOCKS = 16
RNN_BLOCK = 64
RNN_CONV_W = 4
RG_C = 8.0
FFN_CONV_W = 3
EPS = 1e-6
LOG2E = 1.4426950408889634

ZERO_EXP2 = -150.0
BOUND_SLACK = 8.0
NORM_SLACK = 1.05

LANES = 128
MXU_DIM = 256
VMEM_LIMIT_BYTES = 58 * 1024 * 1024
TOKEN_TILE = 512
MLP_CHUNK = 1024

F32 = jnp.float32
BF16 = jnp.bfloat16


def _rms(x, gain):
    return x * lax.rsqrt(jnp.mean(x * x, axis=-1, keepdims=True) + EPS) * gain


def _gelu_tanh(x):
    cdf = 0.5 * (1.0 + jnp.tanh(math.sqrt(2.0 / math.pi) * (x + 0.044715 * (x * x * x))))
    return x * cdf


def _softplus(x):
    return jnp.maximum(x, 0.0) + jnp.log1p(jnp.exp(-jnp.abs(x)))


def _split3(x):
    hi = x.astype(BF16).astype(F32)
    r = x - hi
    mid = r.astype(BF16).astype(F32)
    lo = (r - mid).astype(BF16).astype(F32)
    return hi, mid, lo


def _dot(a, b):
    return jnp.dot(a, b, preferred_element_type=F32)


def _dot_nt(a, b):
    return lax.dot_general(a, b, (((1,), (1,)), ((), ())), preferred_element_type=F32)


def _proj_kernel(x_ref, g_ref, wqt_ref, wvt_ref, wk_ref, wxg_ref, wf_ref, bf_ref, hsel_ref,
                 cw_ref, cb_ref, wbd_ref, ba_ref, bx_ref, lam_ref,
                 qt_ref, k_ref, vt_ref, kb_ref, yr_ref, kst_ref, carry_ref, xtail_ref, hstate_ref):
    tm = x_ref.shape[1]
    first_tile = pl.program_id(1) == 0

    @pl.when(first_tile)
    def _():
        carry_ref[...] = jnp.zeros_like(carry_ref)
        xtail_ref[...] = jnp.zeros_like(xtail_ref)
        hstate_ref[...] = jnp.zeros_like(hstate_ref)

    h = _rms(x_ref[0], g_ref[...]).astype(BF16)
    lane = lax.broadcasted_iota(jnp.int32, (tm, LANES), 1)

    fl = _dot(h, wf_ref[...]) + bf_ref[...]
    logf = jnp.minimum(fl, 0.0) - jnp.log1p(jnp.exp(-jnp.abs(fl)))
    logf = jnp.where(lane < 3 * N_ATTN_HEADS, logf, 0.0)
    hi, mid, lo = _split3(logf)
    pieces = jnp.concatenate([hi, mid, lo], axis=1).astype(BF16)

    xr = _dot(h, wxg_ref[:, 0:D_ATTN])
    gr = _dot(h, wxg_ref[:, D_ATTN:2 * D_ATTN])
    xc = _rglru_conv(xr, cw_ref, cb_ref, xtail_ref)
    yr_ref[0] = _rglru_recurrence(xc, gr, first_tile, wbd_ref, ba_ref, bx_ref, lam_ref,
                                  hstate_ref).astype(yr_ref.dtype)

    qt_ref[0] = (_dot_nt(wqt_ref[0], h) * (HEAD_DIM ** -0.5 * LOG2E)).astype(BF16)

    rows = lax.broadcasted_iota(jnp.int32, (tm, tm), 0)
    cols = lax.broadcasted_iota(jnp.int32, (tm, tm), 1)
    tri = jnp.where(rows >= cols, 1.0, 0.0).astype(BF16)
    cs = _dot(tri, pieces)
    c = cs[:, 0:LANES] + cs[:, LANES:2 * LANES] + cs[:, 2 * LANES:3 * LANES] + carry_ref[...]
    carry_ref[...] = c[tm - 1:tm, :]
    bias = c * (-LOG2E)
    bhi, bmid, blo = _split3(bias)
    kb = jnp.where(lane < N_ATTN_HEADS, bhi, jnp.where(lane < 2 * N_ATTN_HEADS, bmid, blo))
    kb_ref[0] = kb.astype(BF16)

    vt_ref[0] = _dot_nt(wvt_ref[0], h).astype(BF16)
    k = _dot(h, wk_ref[0]).astype(BF16)
    k_ref[0] = k
    k2 = k.astype(F32)
    kn2 = _dot((k2 * k2).astype(BF16), hsel_ref[...])
    kst_ref[0, 0] = jnp.concatenate(
        [jnp.max(kn2, axis=0, keepdims=True), jnp.max(bias, axis=0, keepdims=True),
         jnp.zeros((6, LANES), F32)], axis=0)


def _qkv_weight_kernel(w_ref, wt_ref, wc_ref):
    w = w_ref[...]
    wc_ref[0] = w.astype(BF16)
    wt_ref[0] = w.T.astype(BF16)


def _qkv_weights(w_in, rows=512):
    D = w_in.shape[0]
    return pl.pallas_call(
        _qkv_weight_kernel,
        grid=(3, D // rows),
        in_specs=[pl.BlockSpec((rows, D_ATTN), lambda j, r: (r, j))],
        out_specs=[pl.BlockSpec((1, D_ATTN, rows), lambda j, r: (j, 0, r)),
                   pl.BlockSpec((1, rows, D_ATTN), lambda j, r: (j, r, 0))],
        out_shape=[jax.ShapeDtypeStruct((3, D_ATTN, D), BF16),
                   jax.ShapeDtypeStruct((3, D, D_ATTN), BF16)],
        compiler_params=pltpu.CompilerParams(dimension_semantics=("arbitrary", "arbitrary")),
        name="qkv_weight_prep",
    )(w_in)


def _in_projection(x, gain, w_t, w_c, wxg, wf, bf_rep, hsel, cw, cb, wbd, ba, bx, lam, tm):
    B, S, D = x.shape
    const = lambda a: pl.BlockSpec(a.shape, lambda b, i: (0,) * a.ndim, pipeline_mode=pl.Buffered(1))
    plane = lambda a, n: pl.BlockSpec((1,) + a.shape[1:], lambda b, i: (n, 0, 0),
                                      pipeline_mode=pl.Buffered(1))
    row = lambda width: pl.BlockSpec((1, tm, width), lambda b, i: (b, i, 0))
    col = pl.BlockSpec((1, D_ATTN, tm), lambda b, i: (b, 0, i))
    small = (wxg, wf, bf_rep, hsel, cw, cb, wbd, ba, bx, lam)
    weights = (gain, w_t, w_t, w_c) + small
    d_rnn = cw.shape[1]
    return pl.pallas_call(
        _proj_kernel,
        grid=(B, S // tm),
        in_specs=[row(D), const(gain), plane(w_t, 0), plane(w_t, 2), plane(w_c, 1)]
                 + [const(w) for w in small],
        out_specs=[col, row(D_ATTN), col, row(LANES), row(d_rnn),
                   pl.BlockSpec((1, 1, 8, LANES), lambda b, i: (b, i, 0, 0))],
        out_shape=[jax.ShapeDtypeStruct((B, D_ATTN, S), BF16),
                   jax.ShapeDtypeStruct((B, S, D_ATTN), BF16),
                   jax.ShapeDtypeStruct((B, D_ATTN, S), BF16),
                   jax.ShapeDtypeStruct((B, S, LANES), BF16),
                   jax.ShapeDtypeStruct((B, S, d_rnn), BF16),
                   jax.ShapeDtypeStruct((B, S // tm, 8, LANES), F32)],
        scratch_shapes=[pltpu.VMEM((1, LANES), F32),
                        pltpu.VMEM((8, d_rnn), F32),
                        pltpu.VMEM((1, d_rnn), F32)],
        compiler_params=pltpu.CompilerParams(
            dimension_semantics=("arbitrary", "arbitrary"), vmem_limit_bytes=VMEM_LIMIT_BYTES),
        name="in_projection",
    )(x, *weights)


V_ROWS = HEAD_DIM + 16


def _attn_kernel(qt_ref, k_ref, vt_ref, kb_ref, kst_ref, o_ref,
                 rhs_ref, m_ref, acc_ref, s_a_ref, s_b_ref, s_c_ref, mb_a_ref, mb_b_ref, mb_c_ref,
                 seq_ref, *, tq):
    hp = pl.program_id(1)
    tk = tq
    n_kblocks = k_ref.shape[1] // tk
    n_qblocks = qt_ref.shape[2] // tq
    ones = jnp.ones((V_ROWS - HEAD_DIM, tk), BF16)
    slot_a = (s_a_ref, mb_a_ref)
    slot_b = (s_b_ref, mb_b_ref)
    slot_c = (s_c_ref, mb_c_ref)

    def q_block(i):
        return qt_ref[0, :, pl.ds(pl.multiple_of(i * tq, tq), tq)]

    def scores(j, slot, diagonal, hh):
        s_ref, mb_ref = slot
        start = pl.multiple_of(j * tk, tk)
        lhs = jnp.concatenate([k_ref[0, pl.ds(start, tk), :], kb_ref[0, pl.ds(start, tk), :]], axis=1)
        s = _dot(lhs, rhs_ref[hh])
        if diagonal:
            key = lax.broadcasted_iota(jnp.int32, (tk, tq), 0)
            qry = lax.broadcasted_iota(jnp.int32, (tk, tq), 1)
            s = jnp.where(key <= qry, s, -jnp.inf)
        s_ref[hh] = s
        mb_ref[hh] = jnp.max(s, axis=0, keepdims=True)

    def consume(j, slot, hh):
        s_ref, mb_ref = slot
        start = pl.multiple_of(j * tk, tk)
        m_old = m_ref[hh]
        m_new = jnp.maximum(m_old, mb_ref[hh])
        p = jnp.exp2(s_ref[hh] - m_new).astype(BF16)
        vt = jnp.concatenate(
            [vt_ref[0, hh * HEAD_DIM:(hh + 1) * HEAD_DIM, pl.ds(start, tk)], ones], axis=0)
        acc_ref[hh] = acc_ref[hh] * jnp.exp2(m_old - m_new) + _dot(vt, p)
        m_ref[hh] = m_new

    def stage(js, slot_s, jc, slot_prev):
        for hh in range(2):
            scores(js, slot_s, False, hh)
            consume(jc, slot_prev, hh)

    def fill(i):
        qt = q_block(i)
        row = lax.broadcasted_iota(jnp.int32, (LANES, tq), 0)
        for hh in range(2):
            head = 2 * hp + hh
            qh = jnp.where((row >= hh * HEAD_DIM) & (row < (hh + 1) * HEAD_DIM), qt, jnp.zeros_like(qt))
            sel = (row == head) | (row == head + N_ATTN_HEADS) | (row == head + 2 * N_ATTN_HEADS)
            rhs_ref[hh, 0:LANES, :] = qh
            rhs_ref[hh, LANES:2 * LANES, :] = jnp.where(sel, 1.0, 0.0).astype(BF16)
        for hh in range(2):
            scores(i, slot_c, True, hh)

    def plan(i):
        qt = q_block(i)
        lane = lax.broadcasted_iota(jnp.int32, (1, LANES), 1)
        need = jnp.zeros((1, LANES), jnp.int32)
        for hh in range(2):
            qf = qt[hh * HEAD_DIM:(hh + 1) * HEAD_DIM, :].astype(F32)
            qn2 = jnp.max(jnp.sum(qf * qf, axis=0, keepdims=True), axis=1, keepdims=True)
            m_min = jnp.min(mb_c_ref[hh], axis=1, keepdims=True)
            kn2 = kst_ref[0, 0, hh:hh + 1, :]
            bias_max = kst_ref[0, 0, 2 + hh:3 + hh, :]
            bound = jnp.sqrt(qn2 * kn2 * NORM_SLACK) + bias_max - m_min
            need = need | jnp.where(bound >= ZERO_EXP2 - BOUND_SLACK, 1, 0)
        need = jnp.where(lane < i, need, 0)
        seq_ref[0] = i
        count = jnp.int32(1)
        for j in range(n_kblocks - 2, -1, -1):
            seq_ref[count] = j
            count = count + need[0, j]
        return count - 1

    def finish(i, last, slot):
        for hh in range(2):
            consume(seq_ref[last], slot, hh)
        outs = []
        for hh in range(2):
            acc = acc_ref[hh]
            outs.append(acc[0:HEAD_DIM] * (1.0 / acc[HEAD_DIM:HEAD_DIM + 1]))
        out = jnp.concatenate(outs, axis=0).T
        o_ref[0, pl.ds(pl.multiple_of(i * tq, tq), tq), :] = out.astype(o_ref.dtype)
        fill(jnp.minimum(i + 1, n_qblocks - 1))

    def query_block(i, carry):
        last = plan(i)
        m_ref[...] = jnp.full_like(m_ref, -jnp.inf)
        acc_ref[...] = jnp.zeros_like(acc_ref)

        @pl.when(last == 0)
        def _():
            finish(i, last, slot_c)

        @pl.when(last > 0)
        def _():
            stage(seq_ref[1], slot_b, seq_ref[0], slot_c)
            n_loop = (last - 1) // 2

            def body(t, carry):
                n = 1 + 2 * t
                stage(seq_ref[n + 1], slot_a, seq_ref[n], slot_b)
                stage(seq_ref[n + 2], slot_b, seq_ref[n + 1], slot_a)
                return carry

            lax.fori_loop(0, n_loop, body, 0)
            n = 1 + 2 * n_loop

            @pl.when(last == n)
            def _():
                finish(i, last, slot_b)

            @pl.when(last == n + 1)
            def _():
                stage(seq_ref[last], slot_a, seq_ref[n], slot_b)
                finish(i, last, slot_a)

        return carry

    fill(0)
    lax.fori_loop(0, n_qblocks, query_block, 0)


def _attention(qt, k, vt, kb, kst, tq):
    B, S, _ = k.shape
    n_pairs = N_ATTN_HEADS // 2
    rows = pl.BlockSpec((1, S, LANES), lambda b, p: (b, 0, p))
    cols = pl.BlockSpec((1, LANES, S), lambda b, p: (b, p, 0))
    return pl.pallas_call(
        functools.partial(_attn_kernel, tq=tq),
        grid=(B, n_pairs),
        in_specs=[cols, rows, cols, pl.BlockSpec((1, S, LANES), lambda b, p: (b, 0, 0)),
                  pl.BlockSpec((1, 1, 8, LANES), lambda b, p: (b, p, 0, 0))],
        out_specs=rows,
        out_shape=jax.ShapeDtypeStruct((B, S, D_ATTN), BF16),
        scratch_shapes=[pltpu.VMEM((2, 2 * LANES, tq), BF16),
                        pltpu.VMEM((2, 1, tq), F32),
                        pltpu.VMEM((2, V_ROWS, tq), F32),
                        pltpu.VMEM((2, tq, tq), F32),
                        pltpu.VMEM((2, tq, tq), F32),
                        pltpu.VMEM((2, tq, tq), F32),
                        pltpu.VMEM((2, 1, tq), F32),
                        pltpu.VMEM((2, 1, tq), F32),
                        pltpu.VMEM((2, 1, tq), F32),
                        pltpu.SMEM((S // tq + 1,), jnp.int32)],
        compiler_params=pltpu.CompilerParams(
            dimension_semantics=("arbitrary", "arbitrary"), vmem_limit_bytes=VMEM_LIMIT_BYTES),
        name="fox_attention",
    )(qt, k, vt, kb, kst)


def _rglru_conv(x, cw_ref, cb_ref, xtail_ref):
    ts, width = x.shape
    n_groups = ts // 8
    x3 = x.reshape(n_groups, 8, width)
    tail = xtail_ref[...]
    sub = lax.broadcasted_iota(jnp.int32, (n_groups, 8, width), 1)
    xc3 = cb_ref[...] + cw_ref[RNN_CONV_W - 1:RNN_CONV_W, :] * x3
    for shift in range(1, RNN_CONV_W):
        kk = RNN_CONV_W - 1 - shift
        rot = pltpu.roll(x3, shift, axis=1)
        prev = jnp.concatenate([pltpu.roll(tail, shift, axis=0)[None], rot[:-1]], axis=0)
        xc3 = xc3 + cw_ref[kk:kk + 1, :] * jnp.where(sub < shift, prev, rot)
    xtail_ref[...] = x[ts - 8:ts]
    return xc3.reshape(ts, width)


def _rglru_recurrence(xc, gate, first_tile, wbd_ref, ba_ref, bx_ref, lam_ref, h_ref):
    ts, width = xc.shape
    n_groups = ts // 8
    sub = lax.broadcasted_iota(jnp.int32, (n_groups, 8, width), 1)
    xcb = xc.astype(BF16)
    r_parts, i_parts = [], []
    for g in range(width // MXU_DIM):
        z = _dot(xcb[:, g * MXU_DIM:(g + 1) * MXU_DIM], wbd_ref[g])
        r_parts.append(z[:, 0:MXU_DIM])
        i_parts.append(z[:, MXU_DIM:2 * MXU_DIM])
    r = jax.nn.sigmoid(jnp.concatenate(r_parts, axis=1) + ba_ref[...])
    ig = jax.nn.sigmoid(jnp.concatenate(i_parts, axis=1) + bx_ref[...])
    log_a = (-RG_C) * r * _softplus(-lam_ref[...])
    a = jnp.exp(log_a)
    m2 = jnp.tanh(-log_a) * (1.0 + a * a)
    mult = jnp.where(m2 > 0.0, m2 * lax.rsqrt(m2), 0.0)
    row = lax.broadcasted_iota(jnp.int32, (ts, width), 0)
    mult = jnp.where(row == jnp.where(first_tile, 0, -1), 1.0, mult)
    b = mult * ig * xc

    a3 = a.reshape(n_groups, 8, width)
    b3 = b.reshape(n_groups, 8, width)
    for d in (1, 2, 4):
        valid = sub >= d
        b3 = jnp.where(valid, a3 * pltpu.roll(b3, d, axis=1), 0.0) + b3
        a3 = jnp.where(valid, a3 * pltpu.roll(a3, d, axis=1), a3)
    h_prev = h_ref[...]
    groups = []
    for g in range(n_groups):
        h_g = a3[g] * h_prev + b3[g]
        groups.append(h_g)
        h_prev = h_g[7:8]
    h_ref[...] = h_prev
    return jnp.concatenate(groups, axis=0) * _gelu_tanh(gate)


def _outproj_kernel(ya_ref, yr_ref, x_ref, na_ref, nr_ref, wo_ref, npost_ref, nffn_ref,
                    x1_ref, h2_ref):
    tm, da = ya_ref.shape
    n_parts = 2
    part = tm // n_parts
    for r in range(n_parts):
        rows = pl.ds(r * part, part)
        ya = _rms(ya_ref[rows, :].astype(F32), na_ref[...]).astype(BF16)
        yr = _rms(yr_ref[rows, :].astype(F32), nr_ref[...]).astype(BF16)
        y = _dot(ya, wo_ref[0:da, :]) + _dot(yr, wo_ref[da:, :])
        x1 = x_ref[rows, :] + _rms(y, npost_ref[...])
        x1_ref[rows, :] = x1
        h2_ref[rows, :] = _rms(x1, nffn_ref[...]).astype(BF16)


def _out_projection(ya, yr, x, na, nr, wo, npost, nffn, tm):
    M, D = x.shape
    da, dr = ya.shape[1], yr.shape[1]
    const = lambda shape: pl.BlockSpec(shape, lambda i: (0,) * len(shape),
                                       pipeline_mode=pl.Buffered(1))
    row = lambda width: pl.BlockSpec((tm, width), lambda i: (i, 0))
    return pl.pallas_call(
        _outproj_kernel,
        grid=(M // tm,),
        in_specs=[row(da), row(dr), row(D), const((1, da)), const((1, dr)), const(wo.shape),
                  const((1, D)), const((1, D))],
        out_specs=[row(D), row(D)],
        out_shape=[jax.ShapeDtypeStruct((M, D), F32), jax.ShapeDtypeStruct((M, D), BF16)],
        compiler_params=pltpu.CompilerParams(
            dimension_semantics=("arbitrary",), vmem_limit_bytes=VMEM_LIMIT_BYTES),
        name="out_projection",
    )(ya, yr, x, na, nr, wo, npost, nffn)


def _ffn_kernel(h_ref, wg_ref, wu_ref, cw_ref, cb_ref, wd_ref, x1_ref, gain_ref,
                o_ref, acc_ref, gtail_ref, *, tiles_per_seq):
    tm = h_ref.shape[0]
    tc = wg_ref.shape[1]
    i = pl.program_id(0)
    c = pl.program_id(1)

    @pl.when(c == 0)
    def _():
        acc_ref[...] = jnp.zeros_like(acc_ref)

    h = h_ref[...]
    g = _dot(h, wg_ref[...])
    u = _dot(h, wu_ref[...])
    n_groups = tm // 8
    seq_start = (i % tiles_per_seq) == 0
    tail = jnp.where(seq_start, 0.0, gtail_ref[c])
    gtail_ref[c] = g[tm - 8:tm, :]
    g3 = g.reshape(n_groups, 8, tc)
    sub = lax.broadcasted_iota(jnp.int32, (n_groups, 8, tc), 1)
    conv = cb_ref[...] + cw_ref[FFN_CONV_W - 1:FFN_CONV_W, :] * g3
    for shift in range(1, FFN_CONV_W):
        kk = FFN_CONV_W - 1 - shift
        rot = pltpu.roll(g3, shift, axis=1)
        prev = jnp.concatenate([pltpu.roll(tail, shift, axis=0)[None], rot[:-1]], axis=0)
        conv = conv + cw_ref[kk:kk + 1, :] * jnp.where(sub < shift, prev, rot)
    act = (_gelu_tanh(conv.reshape(tm, tc)) * u).astype(BF16)
    acc_ref[...] += _dot(act, wd_ref[...])

    @pl.when(c == pl.num_programs(1) - 1)
    def _():
        o_ref[...] = x1_ref[...] + _rms(acc_ref[...], gain_ref[...])


def _ffn(h2, wg, wu, cw, cb, wd, x1, gain, tm, tc, tiles_per_seq):
    M, D = h2.shape
    dff = wg.shape[1]
    nc = dff // tc
    return pl.pallas_call(
        functools.partial(_ffn_kernel, tiles_per_seq=tiles_per_seq),
        grid=(M // tm, nc),
        in_specs=[pl.BlockSpec((tm, D), lambda i, c: (i, 0)),
                  pl.BlockSpec((D, tc), lambda i, c: (0, c)),
                  pl.BlockSpec((D, tc), lambda i, c: (0, c)),
                  pl.BlockSpec((FFN_CONV_W, tc), lambda i, c: (0, c)),
                  pl.BlockSpec((1, tc), lambda i, c: (0, c)),
                  pl.BlockSpec((tc, D), lambda i, c: (c, 0)),
                  pl.BlockSpec((tm, D), lambda i, c: (i, 0)),
                  pl.BlockSpec((1, D), lambda i, c: (0, 0))],
        out_specs=pl.BlockSpec((tm, D), lambda i, c: (i, 0)),
        out_shape=jax.ShapeDtypeStruct((M, D), F32),
        scratch_shapes=[pltpu.VMEM((tm, D), F32),
                        pltpu.VMEM((nc, 8, tc), F32)],
        compiler_params=pltpu.CompilerParams(
            dimension_semantics=("arbitrary", "arbitrary"), vmem_limit_bytes=VMEM_LIMIT_BYTES),
        name="conv_gated_mlp",
    )(h2, wg, wu, cw, cb, wd, x1, gain)


def _block_diag(w, group):
    n, r, _ = w.shape
    w = w.reshape(n // group, group, r, r)
    eye = jnp.eye(group, dtype=w.dtype)
    return jnp.einsum("gaij,ab->gaibj", w, eye).reshape(n // group, group * r, group * r)


def _layer(x, norm_mix_pre, w_in, b_forget, conv_rnn_w, conv_rnn_b, w_rg_a, b_rg_a, w_rg_x,
           b_rg_x, rg_lambda, norm_attn_out, norm_rnn_out, w_out, norm_mix_post, norm_ffn_pre,
           w_gate, w_up, conv_ffn_w, conv_ffn_b, w_down, norm_ffn_post):
    B, S, D = x.shape
    d_rnn = conv_rnn_w.shape[1]
    t_seq = min(TOKEN_TILE, S)

    nh = N_ATTN_HEADS
    w_f = w_in[:, 3 * D_ATTN:3 * D_ATTN + nh]
    w_xg = w_in[:, 3 * D_ATTN + nh:]
    w_f_rep = jnp.concatenate([w_f, w_f, w_f, jnp.zeros((D, LANES - 3 * nh), w_in.dtype)], axis=1)
    bf_rep = jnp.concatenate([b_forget, b_forget, b_forget,
                              jnp.zeros((LANES - 3 * nh,), b_forget.dtype)]).reshape(1, LANES)
    group = MXU_DIM // RNN_BLOCK
    wbd = jnp.concatenate([_block_diag(w_rg_a, group), _block_diag(w_rg_x, group)],
                          axis=2).astype(BF16)
    row = lambda v: v.reshape(1, -1)

    hsel = jnp.repeat(jnp.eye(nh, LANES, dtype=BF16), HEAD_DIM, axis=0)
    w_t, w_c = _qkv_weights(w_in[:, 0:3 * D_ATTN])
    qt, k, vt, kb, y_rnn, kst = _in_projection(
        x, row(norm_mix_pre), w_t, w_c, w_xg.astype(BF16), w_f_rep.astype(BF16), bf_rep, hsel,
        conv_rnn_w, row(conv_rnn_b), wbd, row(b_rg_a), row(b_rg_x), row(rg_lambda), t_seq)
    n_blocks = S // t_seq
    kst = kst[:, :, 0:2, 0:nh].reshape(B, n_blocks, 2, nh // 2, 2).transpose(0, 3, 2, 4, 1)
    kst = jnp.pad(kst.reshape(B, nh // 2, 4, n_blocks), ((0, 0), (0, 0), (0, 4), (0, LANES - n_blocks)))
    y_attn = _attention(qt, k, vt, kb, kst, t_seq)
    M = B * S
    x1, h2 = _out_projection(y_attn.reshape(M, D_ATTN), y_rnn.reshape(M, d_rnn), x.reshape(M, D),
                             row(norm_attn_out), row(norm_rnn_out), w_out.astype(BF16),
                             row(norm_mix_post), row(norm_ffn_pre), t_seq)
    out = _ffn(h2, w_gate.astype(BF16), w_up.astype(BF16), conv_ffn_w, row(conv_ffn_b),
               w_down.astype(BF16), x1, row(norm_ffn_post), t_seq, MLP_CHUNK, S // t_seq)
    return out.reshape(B, S, D)


def kernel(x, norm_mix_pre, w_in, b_forget, conv_rnn_w, conv_rnn_b, w_rg_a, b_rg_a, w_rg_x, b_rg_x, rg_lambda, norm_attn_out, norm_rnn_out, w_out, norm_mix_post, norm_ffn_pre, w_gate, w_up, conv_ffn_w, conv_ffn_b, w_down, norm_ffn_post):
    params = (norm_mix_pre, w_in, b_forget, conv_rnn_w, conv_rnn_b, w_rg_a, b_rg_a, w_rg_x, b_rg_x,
              rg_lambda, norm_attn_out, norm_rnn_out, w_out, norm_mix_post, norm_ffn_pre, w_gate,
              w_up, conv_ffn_w, conv_ffn_b, w_down, norm_ffn_post)
    for l in range(norm_mix_pre.shape[0]):
        x = _layer(x, *(p[l] for p in params))
    return x
```
